```python
import jax, jax.numpy as jnp
from jax import lax
import numpy as np

D_MODEL = 2048
BATCH = 1
SEQ = 8192
DEPTH = 1

N_MEM = 256
EPS = 1e-5
D_MIX = D_MODEL
D_A = D_MIX // 2
D_B = D_MIX - D_A
CHUNK = 128
A_GROUPS = 8
A_GROUP_DIM = D_A // A_GROUPS
HEAD_DIM = 64
B_Q_HEADS = D_B // HEAD_DIM
B_KV_HEADS = 2
B_GROUP = B_Q_HEADS // B_KV_HEADS
WINDOW = 128
BLOCK = 128
X_HEADS = 4
X_HEAD_DIM = D_MODEL // X_HEADS
D_FF = 5632
IN_COLS = 2 * D_A + D_B + 2 * B_KV_HEADS * HEAD_DIM
NEG = -1e30

kernel_name = "hybrid_sgu_swa_sink_macaron_layer"


def rmsnorm(x, g):
    xf = x.astype(jnp.float32)
    y = xf * lax.rsqrt(jnp.mean(xf * xf, axis=-1, keepdims=True) + EPS)
    return (y * g.astype(jnp.float32)).astype(x.dtype)


def swiglu(x, w_gate, w_up, w_down):
    return (jax.nn.silu(x @ w_gate) * (x @ w_up)) @ w_down


def spatial_gating(z_uv, g_v, w_s, b_s):
    b, s, _ = z_uv.shape
    z = jax.nn.gelu(z_uv, approximate=False)
    u, v = z[..., :D_A], z[..., D_A:]
    v = rmsnorm(v, g_v)
    v = v.reshape(b, s // CHUNK, CHUNK, A_GROUPS, A_GROUP_DIM)
    causal = jnp.tril(jnp.ones((CHUNK, CHUNK), dtype=bool))
    ws = jnp.where(causal[None], w_s, jnp.zeros_like(w_s))
    sv = jnp.einsum('gts,bnsgc->bntgc', ws, v) + jnp.transpose(b_s)[None, None, :, :, None]
    return u * sv.reshape(b, s, D_A)


def window_attention_sinks(q, k, v, sinks):
    b, s, _ = q.shape
    nb = s // BLOCK
    q = q.reshape(b, nb, BLOCK, B_KV_HEADS, B_GROUP, HEAD_DIM)
    k = k.reshape(b, nb, BLOCK, B_KV_HEADS, HEAD_DIM)
    v = v.reshape(b, nb, BLOCK, B_KV_HEADS, HEAD_DIM)
    pad = ((0, 0), (1, 0), (0, 0), (0, 0), (0, 0))
    kk = jnp.concatenate([jnp.pad(k, pad)[:, :-1], k], axis=2)
    vv = jnp.concatenate([jnp.pad(v, pad)[:, :-1], v], axis=2)
    scores = jnp.einsum('bnqkgd,bnskd->bnkgqs', q, kk,
                        preferred_element_type=jnp.float32) * (HEAD_DIM ** -0.5)
    qpos = jnp.arange(BLOCK)[:, None] + BLOCK
    kpos = jnp.arange(2 * BLOCK)[None, :]
    diff = qpos - kpos
    band = (diff >= 0) & (diff < WINDOW)
    first = (jnp.arange(nb)[:, None, None] == 0) & (kpos[None] < BLOCK)
    mask = band[None] & ~first
    scores = jnp.where(mask[None, :, None, None], scores, NEG)
    sink = jnp.broadcast_to(
        sinks.astype(jnp.float32).reshape(1, 1, B_KV_HEADS, B_GROUP, 1, 1),
        scores.shape[:-1] + (1,))
    probs = jax.nn.softmax(jnp.concatenate([scores, sink], axis=-1), axis=-1)[..., :-1]
    out = jnp.einsum('bnkgqs,bnskd->bnqkgd', probs.astype(vv.dtype), vv)
    return out.reshape(b, s, D_B)


def cross_attention(hn, memn, w_q, w_kv, w_o):
    b, s, _ = hn.shape
    m = memn.shape[1]
    q = (hn @ w_q).reshape(b, s, X_HEADS, X_HEAD_DIM)
    kv = memn @ w_kv
    k = kv[..., :D_MODEL].reshape(b, m, X_HEADS, X_HEAD_DIM)
    v = kv[..., D_MODEL:].reshape(b, m, X_HEADS, X_HEAD_DIM)
    scores = jnp.einsum('bshd,bmhd->bhsm', q, k,
                        preferred_element_type=jnp.float32) * (X_HEAD_DIM ** -0.5)
    probs = jax.nn.softmax(scores, axis=-1).astype(v.dtype)
    out = jnp.einsum('bhsm,bmhd->bshd', probs, v).reshape(b, s, D_MODEL)
    return out @ w_o


def setup_inputs(seed: int = 0) -> dict:
    key = jax.random.key(seed)
    ks = jax.random.split(key, 32)
    L, D, F = DEPTH, D_MODEL, D_FF

    def nrm(k, shape, scale):
        return jax.random.normal(k, shape, dtype=jnp.float32) * scale

    def gain(k, shape):
        return 1.0 + 0.05 * jax.random.normal(k, shape, dtype=jnp.float32)

    return {
        "x": nrm(ks[0], (BATCH, SEQ, D), 1.0),
        "mem": nrm(ks[1], (BATCH, N_MEM, D), 1.0),
        "g_ffn1": gain(ks[2], (L, D)),
        "w1_gate": nrm(ks[3], (L, D, F), D ** -0.5),
        "w1_up": nrm(ks[4], (L, D, F), D ** -0.5),
        "w1_down": nrm(ks[5], (L, F, D), F ** -0.5),
        "g_mix": gain(ks[6], (L, D)),
        "w_in": nrm(ks[7], (L, D, IN_COLS), D ** -0.5),
        "g_v": gain(ks[8], (L, D_A)),
        "w_s": nrm(ks[9], (L, A_GROUPS, CHUNK, CHUNK), 0.5 * CHUNK ** -0.5),
        "b_s": 1.0 + 0.1 * jax.random.normal(ks[10], (L, A_GROUPS, CHUNK), dtype=jnp.float32),
        "sinks": nrm(ks[11], (L, B_Q_HEADS), 0.5),
        "g_a_out": gain(ks[12], (L, D_A)),
        "g_b_out": gain(ks[13], (L, D_B)),
        "w_out": nrm(ks[14], (L, D_MIX, D), D_MIX ** -0.5),
        "g_x": gain(ks[15], (L, D)),
        "g_mem": gain(ks[16], (L, D)),
        "w_xq": nrm(ks[17], (L, D, D), D ** -0.5),
        "w_xkv": nrm(ks[18], (L, D, 2 * D), D ** -0.5),
        "w_xo": nrm(ks[19], (L, D, D), D ** -0.5),
        "g_ffn2": gain(ks[20], (L, D)),
        "w2_gate": nrm(ks[21], (L, D, F), D ** -0.5),
        "w2_up": nrm(ks[22], (L, D, F), D ** -0.5),
        "w2_down": nrm(ks[23], (L, F, D), F ** -0.5),
        "g_final": gain(ks[24], (D,)),
    }


def reference(x, mem, g_ffn1, w1_gate, w1_up, w1_down, g_mix, w_in, g_v, w_s, b_s,
              sinks, g_a_out, g_b_out, w_out, g_x, g_mem, w_xq, w_xkv, w_xo,
              g_ffn2, w2_gate, w2_up, w2_down, g_final):
    o_q = 2 * D_A
    o_k = o_q + D_B
    o_v = o_k + B_KV_HEADS * HEAD_DIM
    h = x
    for l in range(DEPTH):
        h = h + 0.5 * swiglu(rmsnorm(h, g_ffn1[l]), w1_gate[l], w1_up[l], w1_down[l])
        z = rmsnorm(h, g_mix[l]) @ w_in[l]
        y_a = spatial_gating(z[..., :o_q], g_v[l], w_s[l], b_s[l])
        y_b = window_attention_sinks(z[..., o_q:o_k], z[..., o_k:o_v], z[..., o_v:], sinks[l])
        y = jnp.concatenate([rmsnorm(y_a, g_a_out[l]), rmsnorm(y_b, g_b_out[l])], axis=-1)
        h = h + y @ w_out[l]
        h = h + cross_attention(rmsnorm(h, g_x[l]), rmsnorm(mem, g_mem[l]),
                                w_xq[l], w_xkv[l], w_xo[l])
        h = h + 0.5 * swiglu(rmsnorm(h, g_ffn2[l]), w2_gate[l], w2_up[l], w2_down[l])
    return rmsnorm(h, g_final)
```

```python
import functools

import jax
import jax.numpy as jnp
from jax import lax
from jax.experimental import pallas as pl
from jax.experimental.pallas import tpu as pltpu

F32 = jnp.float32
BF16 = jnp.bfloat16

EPS = 1e-5
NEG = -1e30
CHUNK = 128
A_GROUPS = 8
HEAD_DIM = 64
B_KV_HEADS = 2
X_HEADS = 4

V7X_VMEM_BYTES = 64 * 2**20
V7X_LANES = 128
V7X_MXU_N = 256


def _vmem_limit(*byte_counts):
    return int(min(sum(byte_counts), V7X_VMEM_BYTES))


def _nbytes(shape, dtype):
    n = 1
    for s in shape:
        n *= s
    return n * jnp.dtype(dtype).itemsize


def _rmsnorm(x, g):
    ms = jnp.mean(x * x, axis=-1, keepdims=True)
    return x * lax.rsqrt(ms + EPS) * g


def _gelu(x):
    return 0.5 * x * (1.0 + lax.erf(x * (2.0 ** -0.5)))


def _dot(a, b):
    return jnp.dot(a, b, preferred_element_type=F32)


def _dot_nt(a, b):
    return lax.dot_general(a, b, (((1,), (1,)), ((), ())), preferred_element_type=F32)


_ARB1 = ("arbitrary",)
_ARB2 = ("arbitrary", "arbitrary")


def _ffn_up_kernel(h_ref, g_ref, wg_ref, wu_ref, act_ref, hn_ref):
    @pl.when(pl.program_id(1) == 0)
    def _():
        hn_ref[...] = _rmsnorm(h_ref[...], g_ref[...]).astype(BF16)

    hn = hn_ref[...]
    for c in range(0, act_ref.shape[1], V7X_MXU_N):
        cols = slice(c, c + V7X_MXU_N)
        gate = _dot(hn, wg_ref[:, cols].astype(BF16))
        up = _dot(hn, wu_ref[:, cols].astype(BF16))
        act_ref[:, cols] = ((0.5 * jax.nn.silu(gate)) * up).astype(BF16)


def _ffn_up(h, g, wg, wu, *, tm=1024, tn=512):
    s, d = h.shape
    f = wg.shape[1]
    limit = _vmem_limit(
        2 * _nbytes((tm, d), F32), _nbytes((tm, d), BF16),
        2 * 2 * _nbytes((d, tn), F32), 2 * _nbytes((tm, tn), BF16),
        2 * _nbytes((d, tn), BF16), 4 * _nbytes((tm, tn), F32),
        _nbytes((tm, d), F32))
    return pl.pallas_call(
        _ffn_up_kernel,
        grid=(s // tm, f // tn),
        in_specs=[
            pl.BlockSpec((tm, d), lambda i, j: (i, 0)),
            pl.BlockSpec((1, d), lambda i, j: (0, 0)),
            pl.BlockSpec((d, tn), lambda i, j: (0, j)),
            pl.BlockSpec((d, tn), lambda i, j: (0, j)),
        ],
        out_specs=pl.BlockSpec((tm, tn), lambda i, j: (i, j)),
        out_shape=jax.ShapeDtypeStruct((s, f), BF16),
        scratch_shapes=[pltpu.VMEM((tm, d), BF16)],
        compiler_params=pltpu.CompilerParams(dimension_semantics=_ARB2, vmem_limit_bytes=limit),
        name="ffn_up",
    )(h, g, wg, wu)


def _ffn_down_kernel(act_ref, wd_ref, h_ref, o_ref):
    o_ref[...] = h_ref[...] + _dot(act_ref[...], wd_ref[...].astype(BF16))


def _ffn_down(act, wd, h, *, tm=1024, tn=256):
    s, f = act.shape
    d = wd.shape[1]
    limit = _vmem_limit(
        2 * _nbytes((tm, f), BF16), 2 * _nbytes((f, tn), F32), _nbytes((f, tn), BF16),
        4 * _nbytes((tm, tn), F32), 2 * _nbytes((tm, tn), F32))
    return pl.pallas_call(
        _ffn_down_kernel,
        grid=(s // tm, d // tn),
        in_specs=[
            pl.BlockSpec((tm, f), lambda i, j: (i, 0)),
            pl.BlockSpec((f, tn), lambda i, j: (0, j)),
            pl.BlockSpec((tm, tn), lambda i, j: (i, j)),
        ],
        out_specs=pl.BlockSpec((tm, tn), lambda i, j: (i, j)),
        out_shape=jax.ShapeDtypeStruct((s, d), F32),
        compiler_params=pltpu.CompilerParams(dimension_semantics=_ARB2, vmem_limit_bytes=limit),
        name="ffn_down",
    )(act, wd, h)


def _in_proj_kernel(h_ref, g_ref, w_ref, u_ref, v_ref, q_ref, kv_ref):
    hn = _rmsnorm(h_ref[...], g_ref[...]).astype(BF16)
    d_a = u_ref.shape[1]
    d_b = q_ref.shape[1]

    def proj(lo):
        return _dot(hn, w_ref[:, lo:lo + V7X_MXU_N])

    for c in range(0, d_a, V7X_MXU_N):
        u_ref[:, c:c + V7X_MXU_N] = _gelu(proj(c)).astype(BF16)
    for c in range(0, d_a, V7X_MXU_N):
        v_ref[:, c:c + V7X_MXU_N] = _gelu(proj(d_a + c)).astype(BF16)
    for c in range(0, d_b, V7X_MXU_N):
        q_ref[:, c:c + V7X_MXU_N] = proj(2 * d_a + c).astype(BF16)
    for c in range(0, kv_ref.shape[1], V7X_MXU_N):
        kv_ref[:, c:c + V7X_MXU_N] = proj(2 * d_a + d_b + c).astype(BF16)


def _in_proj(h, g, w, d_a, d_b, d_kv, *, tm=1024):
    s, d = h.shape
    n = w.shape[1]
    limit = _vmem_limit(
        2 * _nbytes((tm, d), F32), _nbytes((d, n), BF16), 2 * _nbytes((tm, n), BF16),
        _nbytes((tm, d), F32), _nbytes((tm, d), BF16), 6 * _nbytes((tm, V7X_MXU_N), F32))
    return pl.pallas_call(
        _in_proj_kernel,
        grid=(s // tm,),
        in_specs=[
            pl.BlockSpec((tm, d), lambda i: (i, 0)),
            pl.BlockSpec((1, d), lambda i: (0, 0)),
            pl.BlockSpec((d, n), lambda i: (0, 0), pipeline_mode=pl.Buffered(1)),
        ],
        out_specs=[
            pl.BlockSpec((tm, d_a), lambda i: (i, 0)),
            pl.BlockSpec((tm, d_a), lambda i: (i, 0)),
            pl.BlockSpec((tm, d_b), lambda i: (i, 0)),
            pl.BlockSpec((tm, d_kv), lambda i: (i, 0)),
        ],
        out_shape=[
            jax.ShapeDtypeStruct((s, d_a), BF16),
            jax.ShapeDtypeStruct((s, d_a), BF16),
            jax.ShapeDtypeStruct((s, d_b), BF16),
            jax.ShapeDtypeStruct((s, d_kv), BF16),
        ],
        compiler_params=pltpu.CompilerParams(dimension_semantics=_ARB1, vmem_limit_bytes=limit),
        name="in_proj",
    )(h, g, w)


def _mixer_kernel(sinks_ref, u_ref, v_ref, q_ref, kvc_ref, kvp_ref, h_ref, gv_ref, ws_ref, bst_ref,
                  ga_ref, gb_ref, wout_ref, o_ref, kvbuf_ref, wsm_ref, y_ref):
    tm, d_a = u_ref.shape
    d_b = q_ref.shape[1]
    n_blocks = tm // CHUNK
    group = d_b // HEAD_DIM // B_KV_HEADS
    kv_w = B_KV_HEADS * HEAD_DIM
    tile = pl.program_id(0)

    kvbuf_ref[0:CHUNK, :] = kvp_ref[...]
    kvbuf_ref[CHUNK:, :] = kvc_ref[...]

    t_idx = lax.broadcasted_iota(jnp.int32, (CHUNK, CHUNK), 0)
    s_idx = lax.broadcasted_iota(jnp.int32, (CHUNK, CHUNK), 1)
    for g in range(A_GROUPS):
        wsm_ref[g] = jnp.where(s_idx <= t_idx, ws_ref[g], 0.0).astype(BF16)

    lane = lax.broadcasted_iota(jnp.int32, (CHUNK, V7X_LANES), 1)
    low_half = lane < HEAD_DIM
    rows = group * CHUNK
    q_pos = lax.broadcasted_iota(jnp.int32, (rows, 2 * CHUNK), 0) & (CHUNK - 1)
    k_pos = lax.broadcasted_iota(jnp.int32, (rows, 2 * CHUNK), 1)
    band = (k_pos > q_pos) & (k_pos <= q_pos + CHUNK)

    def block(b, carry):
        r0 = pl.multiple_of(b * CHUNK, CHUNK)
        blk = pl.ds(r0, CHUNK)

        vn = _rmsnorm(v_ref[blk, :].astype(F32), gv_ref[...]).astype(BF16)
        ub = u_ref[blk, :].astype(F32)
        ya = []
        for g in range(A_GROUPS):
            cols = slice(g * CHUNK, (g + 1) * CHUNK)
            sv = _dot(wsm_ref[g], vn[:, cols]) + bst_ref[:, g:g + 1]
            ya.append(ub[:, cols] * sv)
        ya = jnp.concatenate(ya, axis=1)
        y_ref[blk, 0:d_a] = _rmsnorm(ya, ga_ref[...]).astype(BF16)

        qb = q_ref[blk, :] * (HEAD_DIM ** -0.5)
        kk = kvbuf_ref[pl.ds(r0, 2 * CHUNK), :]
        k_f = kk[:, 0:kv_w].astype(F32)
        v_f = kk[:, kv_w:2 * kv_w].astype(F32)
        k_rot = pltpu.roll(k_f, HEAD_DIM, axis=1)
        v_rot = pltpu.roll(v_f, HEAD_DIM, axis=1)
        low2 = lax.broadcasted_iota(jnp.int32, (2 * CHUNK, V7X_LANES), 1) < HEAD_DIM
        first = jnp.logical_and(tile == 0, b == 0)
        k_lo = jnp.where(first, CHUNK, 0)
        mask = band & (k_pos >= k_lo)
        yb = []
        for j in range(B_KV_HEADS):
            if j == 0:
                k2 = jnp.where(low2, k_f, k_rot).astype(BF16)
                v2 = jnp.where(low2, v_f, v_rot).astype(BF16)
            else:
                k2 = jnp.where(low2, k_rot, k_f).astype(BF16)
                v2 = jnp.where(low2, v_rot, v_f).astype(BF16)
            qm = []
            sink = []
            for hh in range(group):
                head = j * group + hh
                c0 = (head // 2) * V7X_LANES
                qc = qb[:, c0:c0 + V7X_LANES]
                keep = low_half if head % 2 == 0 else jnp.logical_not(low_half)
                qm.append(jnp.where(keep, qc, jnp.zeros_like(qc)))
                sink.append(jnp.full((CHUNK, 1), sinks_ref[head], F32))
            qm = jnp.concatenate(qm, axis=0)
            sink = jnp.concatenate(sink, axis=0)
            sc = jnp.where(mask, _dot_nt(qm, k2), NEG)
            m = jnp.maximum(jnp.max(sc, axis=1, keepdims=True), sink)
            e = jnp.exp(sc - m)
            denom = jnp.sum(e, axis=1, keepdims=True) + jnp.exp(sink - m)
            p = (e * (1.0 / denom)).astype(BF16)
            o = _dot(p, v2)
            for pair in range(group // 2):
                even = o[(2 * pair) * CHUNK:(2 * pair + 1) * CHUNK, :]
                odd = o[(2 * pair + 1) * CHUNK:(2 * pair + 2) * CHUNK, :]
                yb.append(jnp.where(low_half, even, odd))
        yb = jnp.concatenate(yb, axis=1)
        y_ref[blk, d_a:d_a + d_b] = _rmsnorm(yb, gb_ref[...]).astype(BF16)
        return carry

    lax.fori_loop(0, n_blocks, block, 0)

    o_ref[...] = h_ref[...] + _dot(y_ref[...], wout_ref[...])


def _mixer(sinks, u, v, q, kv, h, g_v, w_s, b_s_t, g_a, g_b, w_out, *, tm=512):
    s, d = h.shape
    d_a = u.shape[1]
    d_b = q.shape[1]
    d_kv = kv.shape[1]
    d_mix = d_a + d_b
    n_blocks = tm // CHUNK
    limit = _vmem_limit(
        2 * 2 * _nbytes((tm, d_a), BF16), 2 * _nbytes((tm, d_b), BF16),
        2 * _nbytes((tm + CHUNK, d_kv), BF16), 4 * _nbytes((tm, d), F32),
        _nbytes((d_mix, d), BF16), _nbytes((tm, d_mix), BF16), _nbytes((tm + CHUNK, d_kv), BF16),
        2 * _nbytes((A_GROUPS, CHUNK, CHUNK), F32), 2 * _nbytes((tm, d), F32),
        16 * _nbytes((CHUNK, d_a), F32))
    vec = lambda n: pl.BlockSpec((1, n), lambda i: (0, 0))
    return pl.pallas_call(
        _mixer_kernel,
        grid=(s // tm,),
        in_specs=[
            pl.BlockSpec(memory_space=pltpu.SMEM),
            pl.BlockSpec((tm, d_a), lambda i: (i, 0)),
            pl.BlockSpec((tm, d_a), lambda i: (i, 0)),
            pl.BlockSpec((tm, d_b), lambda i: (i, 0)),
            pl.BlockSpec((tm, d_kv), lambda i: (i, 0)),
            pl.BlockSpec((CHUNK, d_kv), lambda i: (jnp.maximum(i * n_blocks - 1, 0), 0)),
            pl.BlockSpec((tm, d), lambda i: (i, 0)),
            vec(d_a),
            pl.BlockSpec((A_GROUPS, CHUNK, CHUNK), lambda i: (0, 0, 0)),
            pl.BlockSpec((CHUNK, A_GROUPS), lambda i: (0, 0)),
            vec(d_a),
            vec(d_b),
            pl.BlockSpec((d_mix, d), lambda i: (0, 0), pipeline_mode=pl.Buffered(1)),
        ],
        out_specs=pl.BlockSpec((tm, d), lambda i: (i, 0)),
        out_shape=jax.ShapeDtypeStruct((s, d), F32),
        scratch_shapes=[
            pltpu.VMEM((tm + CHUNK, d_kv), BF16),
            pltpu.VMEM((A_GROUPS, CHUNK, CHUNK), BF16),
            pltpu.VMEM((tm, d_mix), BF16),
        ],
        compiler_params=pltpu.CompilerParams(dimension_semantics=_ARB1, vmem_limit_bytes=limit),
        name="mixer",
    )(sinks, u, v, q, kv, kv, h, g_v, w_s, b_s_t, g_a, g_b, w_out)


def _mem_kv_kernel(mem_ref, g_ref, w_ref, kv_ref):
    mn = _rmsnorm(mem_ref[...], g_ref[...]).astype(BF16)
    kv_ref[...] = _dot(mn, w_ref[...].astype(BF16)).astype(BF16)


def _mem_kv(mem, g, w, *, tn=512):
    m, d = mem.shape
    n = w.shape[1]
    limit = _vmem_limit(
        2 * _nbytes((m, d), F32), 2 * _nbytes((d, tn), F32), _nbytes((d, tn), BF16),
        2 * _nbytes((m, tn), BF16), 3 * _nbytes((m, d), F32), 2 * _nbytes((m, tn), F32))
    return pl.pallas_call(
        _mem_kv_kernel,
        grid=(n // tn,),
        in_specs=[
            pl.BlockSpec((m, d), lambda j: (0, 0)),
            pl.BlockSpec((1, d), lambda j: (0, 0)),
            pl.BlockSpec((d, tn), lambda j: (0, j)),
        ],
        out_specs=pl.BlockSpec((m, tn), lambda j: (0, j)),
        out_shape=jax.ShapeDtypeStruct((m, n), BF16),
        compiler_params=pltpu.CompilerParams(dimension_semantics=_ARB1, vmem_limit_bytes=limit),
        name="mem_kv",
    )(mem, g, w)


def _cross_kernel(h_ref, g_ref, wq_ref, kv_ref, wo_ref, o_ref):
    x = h_ref[...]
    d = x.shape[1]
    hd = d // X_HEADS
    hn = _rmsnorm(x, g_ref[...]).astype(BF16)
    outs = []
    for head in range(X_HEADS):
        cols = slice(head * hd, (head + 1) * hd)
        q = _dot(hn, wq_ref[:, cols]).astype(BF16)
        sc = _dot_nt(q, kv_ref[:, cols]) * (hd ** -0.5)
        m = jnp.max(sc, axis=1, keepdims=True)
        e = jnp.exp(sc - m)
        p = (e * (1.0 / jnp.sum(e, axis=1, keepdims=True))).astype(BF16)
        outs.append(_dot(p, kv_ref[:, d + head * hd:d + (head + 1) * hd]).astype(BF16))
    o_ref[...] = x + _dot(jnp.concatenate(outs, axis=1), wo_ref[...])


def _cross(h, g, wq, kv, wo, *, tm=512):
    s, d = h.shape
    m = kv.shape[0]
    limit = _vmem_limit(
        4 * _nbytes((tm, d), F32), 2 * _nbytes((d, d), BF16), 2 * _nbytes((m, 2 * d), BF16),
        3 * _nbytes((tm, d), F32), 2 * _nbytes((tm, d), BF16), 4 * _nbytes((tm, d // X_HEADS), F32))
    const = lambda shape: pl.BlockSpec(shape, lambda i: (0, 0), pipeline_mode=pl.Buffered(1))
    return pl.pallas_call(
        _cross_kernel,
        grid=(s // tm,),
        in_specs=[
            pl.BlockSpec((tm, d), lambda i: (i, 0)),
            pl.BlockSpec((1, d), lambda i: (0, 0)),
            const((d, d)),
            pl.BlockSpec((m, 2 * d), lambda i: (0, 0)),
            const((d, d)),
        ],
        out_specs=pl.BlockSpec((tm, d), lambda i: (i, 0)),
        out_shape=jax.ShapeDtypeStruct((s, d), F32),
        compiler_params=pltpu.CompilerParams(dimension_semantics=_ARB1, vmem_limit_bytes=limit),
        name="cross_attn",
    )(h, g, wq, kv, wo)


def _final_norm_kernel(h_ref, g_ref, o_ref):
    o_ref[...] = _rmsnorm(h_ref[...], g_ref[...])


def _final_norm(h, g, *, tm=1024):
    s, d = h.shape
    limit = _vmem_limit(4 * _nbytes((tm, d), F32), 2 * _nbytes((tm, d), F32))
    return pl.pallas_call(
        _final_norm_kernel,
        grid=(s // tm,),
        in_specs=[pl.BlockSpec((tm, d), lambda i: (i, 0)), pl.BlockSpec((1, d), lambda i: (0, 0))],
        out_specs=pl.BlockSpec((tm, d), lambda i: (i, 0)),
        out_shape=jax.ShapeDtypeStruct((s, d), F32),
        compiler_params=pltpu.CompilerParams(dimension_semantics=_ARB1, vmem_limit_bytes=limit),
        name="final_norm",
    )(h, g)


def _row(v):
    return v.reshape(1, -1)


def kernel(x, mem, g_ffn1, w1_gate, w1_up, w1_down, g_mix, w_in, g_v, w_s, b_s, sinks, g_a_out,
           g_b_out, w_out, g_x, g_mem, w_xq, w_xkv, w_xo, g_ffn2, w2_gate, w2_up, w2_down, g_final):
    depth = g_ffn1.shape[0]
    d_a = g_v.shape[1]
    d_b = g_b_out.shape[1]
    d_kv = 2 * B_KV_HEADS * HEAD_DIM
    outs = []
    for b in range(x.shape[0]):
        h = x[b]
        for l in range(depth):
            act = _ffn_up(h, _row(g_ffn1[l]), w1_gate[l], w1_up[l])
            h = _ffn_down(act, w1_down[l], h)

            u, v, q, kv = _in_proj(h, _row(g_mix[l]), w_in[l].astype(BF16), d_a, d_b, d_kv)
            h = _mixer(sinks[l], u, v, q, kv, h, _row(g_v[l]), w_s[l], jnp.transpose(b_s[l]),
                       _row(g_a_out[l]), _row(g_b_out[l]), w_out[l].astype(BF16))

            mem_kv = _mem_kv(mem[b], _row(g_mem[l]), w_xkv[l])
            h = _cross(h, _row(g_x[l]), w_xq[l].astype(BF16), mem_kv, w_xo[l].astype(BF16))

            act = _ffn_up(h, _row(g_ffn2[l]), w2_gate[l], w2_up[l])
            h = _ffn_down(act, w2_down[l], h)
        outs.append(_final_norm(h, _row(g_final)))
    return outs[0][None] if len(outs) == 1 else jnp.stack(outs, axis=0)
```

```python
import functools

import jax
import jax.numpy as jnp
from jax import lax
from jax.experimental import pallas as pl
from jax.experimental.pallas import tpu as pltpu

F32 = jnp.float32
BF16 = jnp.bfloat16

EPS = 1e-5
NEG = -1e30
CHUNK = 128
A_GROUPS = 8
HEAD_DIM = 64
B_KV_HEADS = 2
X_HEADS = 4

V7X_VMEM_BYTES = 64 * 2**20
V7X_LANES = 128
V7X_MXU_N = 256


def _vmem_limit(*byte_counts):
    return int(min(sum(byte_counts), V7X_VMEM_BYTES))


def _nbytes(shape, dtype):
    n = 1
    for s in shape:
        n *= s
    return n * jnp.dtype(dtype).itemsize


def _rmsnorm(x, g):
    ms = jnp.mean(x * x, axis=-1, keepdims=True)
    return x * lax.rsqrt(ms + EPS) * g


def _gelu(x):
    return 0.5 * x * (1.0 + lax.erf(x * (2.0 ** -0.5)))


def _dot(a, b):
    return jnp.dot(a, b, preferred_element_type=F32)


def _dot_nt(a, b):
    return lax.dot_general(a, b, (((1,), (1,)), ((), ())), preferred_element_type=F32)


_ARB1 = ("arbitrary",)
_ARB2 = ("arbitrary", "arbitrary")


def _ffn_up_kernel(h_ref, g_ref, wg_ref, wu_ref, act_ref, hn_ref):
    @pl.when(pl.program_id(1) == 0)
    def _():
        hn_ref[...] = _rmsnorm(h_ref[...], g_ref[...]).astype(BF16)

    hn = hn_ref[...]
    for c in range(0, act_ref.shape[1], V7X_MXU_N):
        cols = slice(c, c + V7X_MXU_N)
        gate = _dot(hn, wg_ref[:, cols].astype(BF16))
        up = _dot(hn, wu_ref[:, cols].astype(BF16))
        act_ref[:, cols] = ((0.5 * jax.nn.silu(gate)) * up).astype(BF16)


def _ffn_up(h, g, wg, wu, *, tm=1024, tn=512):
    s, d = h.shape
    f = wg.shape[1]
    limit = _vmem_limit(
        2 * _nbytes((tm, d), F32), _nbytes((tm, d), BF16),
        2 * 2 * _nbytes((d, tn), wg.dtype), 2 * _nbytes((tm, tn), BF16),
        2 * _nbytes((d, tn), BF16), 4 * _nbytes((tm, tn), F32),
        _nbytes((tm, d), F32))
    return pl.pallas_call(
        _ffn_up_kernel,
        grid=(s // tm, f // tn),
        in_specs=[
            pl.BlockSpec((tm, d), lambda i, j: (i, 0)),
            pl.BlockSpec((1, d), lambda i, j: (0, 0)),
            pl.BlockSpec((d, tn), lambda i, j: (0, j)),
            pl.BlockSpec((d, tn), lambda i, j: (0, j)),
        ],
        out_specs=pl.BlockSpec((tm, tn), lambda i, j: (i, j)),
        out_shape=jax.ShapeDtypeStruct((s, f), BF16),
        scratch_shapes=[pltpu.VMEM((tm, d), BF16)],
        compiler_params=pltpu.CompilerParams(dimension_semantics=_ARB2, vmem_limit_bytes=limit),
        name="ffn_up",
    )(h, g, wg, wu)


def _ffn_down_kernel(act_ref, wd_ref, h_ref, o_ref):
    o_ref[...] = h_ref[...] + _dot(act_ref[...], wd_ref[...].astype(BF16))


def _ffn_down(act, wd, h, *, tm=1024, tn=512):
    s, f = act.shape
    d = wd.shape[1]
    limit = _vmem_limit(
        2 * _nbytes((tm, f), BF16), 2 * _nbytes((f, tn), wd.dtype), _nbytes((f, tn), BF16),
        4 * _nbytes((tm, tn), F32), 2 * _nbytes((tm, tn), F32))
    return pl.pallas_call(
        _ffn_down_kernel,
        grid=(s // tm, d // tn),
        in_specs=[
            pl.BlockSpec((tm, f), lambda i, j: (i, 0)),
            pl.BlockSpec((f, tn), lambda i, j: (0, j)),
            pl.BlockSpec((tm, tn), lambda i, j: (i, j)),
        ],
        out_specs=pl.BlockSpec((tm, tn), lambda i, j: (i, j)),
        out_shape=jax.ShapeDtypeStruct((s, d), F32),
        compiler_params=pltpu.CompilerParams(dimension_semantics=_ARB2, vmem_limit_bytes=limit),
        name="ffn_down",
    )(act, wd, h)


def _in_proj_kernel(h_ref, g_ref, w_ref, u_ref, v_ref, q_ref, kv_ref):
    hn = _rmsnorm(h_ref[...], g_ref[...]).astype(BF16)
    d_a = u_ref.shape[1]
    d_b = q_ref.shape[1]

    def proj(lo):
        return _dot(hn, w_ref[:, lo:lo + V7X_MXU_N])

    for c in range(0, d_a, V7X_MXU_N):
        u_ref[:, c:c + V7X_MXU_N] = _gelu(proj(c)).astype(BF16)
    for c in range(0, d_a, V7X_MXU_N):
        v_ref[:, c:c + V7X_MXU_N] = _gelu(proj(d_a + c)).astype(BF16)
    for c in range(0, d_b, V7X_MXU_N):
        q_ref[:, c:c + V7X_MXU_N] = proj(2 * d_a + c).astype(BF16)
    for c in range(0, kv_ref.shape[1], V7X_MXU_N):
        kv_ref[:, c:c + V7X_MXU_N] = proj(2 * d_a + d_b + c).astype(BF16)


def _in_proj(h, g, w, d_a, d_b, d_kv, *, tm=1024):
    s, d = h.shape
    n = w.shape[1]
    limit = _vmem_limit(
        2 * _nbytes((tm, d), F32), _nbytes((d, n), BF16), 2 * _nbytes((tm, n), BF16),
        _nbytes((tm, d), F32), _nbytes((tm, d), BF16), 6 * _nbytes((tm, V7X_MXU_N), F32))
    return pl.pallas_call(
        _in_proj_kernel,
        grid=(s // tm,),
        in_specs=[
            pl.BlockSpec((tm, d), lambda i: (i, 0)),
            pl.BlockSpec((1, d), lambda i: (0, 0)),
            pl.BlockSpec((d, n), lambda i: (0, 0), pipeline_mode=pl.Buffered(1)),
        ],
        out_specs=[
            pl.BlockSpec((tm, d_a), lambda i: (i, 0)),
            pl.BlockSpec((tm, d_a), lambda i: (i, 0)),
            pl.BlockSpec((tm, d_b), lambda i: (i, 0)),
            pl.BlockSpec((tm, d_kv), lambda i: (i, 0)),
        ],
        out_shape=[
            jax.ShapeDtypeStruct((s, d_a), BF16),
            jax.ShapeDtypeStruct((s, d_a), BF16),
            jax.ShapeDtypeStruct((s, d_b), BF16),
            jax.ShapeDtypeStruct((s, d_kv), BF16),
        ],
        compiler_params=pltpu.CompilerParams(dimension_semantics=_ARB1, vmem_limit_bytes=limit),
        name="in_proj",
    )(h, g, w)


def _mixer_kernel(sinks_ref, u_ref, v_ref, q_ref, kvc_ref, kvp_ref, h_ref, gv_ref, ws_ref, bst_ref,
                  ga_ref, gb_ref, wout_ref, o_ref, kvbuf_ref, wsm_ref, y_ref):
    tm, d_a = u_ref.shape
    d_b = q_ref.shape[1]
    n_blocks = tm // CHUNK
    group = d_b // HEAD_DIM // B_KV_HEADS
    kv_w = B_KV_HEADS * HEAD_DIM
    tile = pl.program_id(0)

    kvbuf_ref[0:CHUNK, :] = kvp_ref[...]
    kvbuf_ref[CHUNK:, :] = kvc_ref[...]

    t_idx = lax.broadcasted_iota(jnp.int32, (CHUNK, CHUNK), 0)
    s_idx = lax.broadcasted_iota(jnp.int32, (CHUNK, CHUNK), 1)
    for g in range(A_GROUPS):
        wsm_ref[g] = jnp.where(s_idx <= t_idx, ws_ref[g], 0.0).astype(BF16)

    low_half = lax.broadcasted_iota(jnp.int32, (CHUNK, V7X_LANES), 1) < HEAD_DIM
    low2 = lax.broadcasted_iota(jnp.int32, (2 * CHUNK, V7X_LANES), 1) < HEAD_DIM
    top_half = lax.broadcasted_iota(jnp.int32, (V7X_LANES, CHUNK), 0) < HEAD_DIM
    k_pos = lax.broadcasted_iota(jnp.int32, (2 * CHUNK, CHUNK), 0)
    q_pos = lax.broadcasted_iota(jnp.int32, (2 * CHUNK, CHUNK), 1)
    band = (k_pos > q_pos) & (k_pos <= q_pos + CHUNK)

    def block(b, carry):
        r0 = pl.multiple_of(b * CHUNK, CHUNK)
        blk = pl.ds(r0, CHUNK)

        vn = _rmsnorm(v_ref[blk, :].astype(F32), gv_ref[...]).astype(BF16)
        ub = u_ref[blk, :].astype(F32)
        ya = []
        for g in range(A_GROUPS):
            cols = slice(g * CHUNK, (g + 1) * CHUNK)
            sv = _dot(wsm_ref[g], vn[:, cols]) + bst_ref[:, g:g + 1]
            ya.append(ub[:, cols] * sv)
        ya = jnp.concatenate(ya, axis=1)
        y_ref[blk, 0:d_a] = _rmsnorm(ya, ga_ref[...]).astype(BF16)

        qb = q_ref[blk, :] * (HEAD_DIM ** -0.5)
        kk = kvbuf_ref[pl.ds(r0, 2 * CHUNK), :]
        k_f = kk[:, 0:kv_w].astype(F32)
        v_f = kk[:, kv_w:2 * kv_w].astype(F32)
        k_rot = pltpu.roll(k_f, HEAD_DIM, axis=1)
        v_rot = pltpu.roll(v_f, HEAD_DIM, axis=1)
        first = jnp.logical_and(tile == 0, b == 0)
        mask = band & (k_pos >= jnp.where(first, CHUNK, 0))
        ybt = []
        for j in range(B_KV_HEADS):
            if j == 0:
                k2 = jnp.where(low2, k_f, k_rot).astype(BF16)
                v2t = jnp.where(low2, v_f, v_rot).T.astype(BF16)
            else:
                k2 = jnp.where(low2, k_rot, k_f).astype(BF16)
                v2t = jnp.where(low2, v_rot, v_f).T.astype(BF16)
            qm = []
            for hh in range(group):
                head = j * group + hh
                c0 = (head // 2) * V7X_LANES
                qc = qb[:, c0:c0 + V7X_LANES]
                keep = low_half if head % 2 == 0 else jnp.logical_not(low_half)
                qm.append(jnp.where(keep, qc, jnp.zeros_like(qc)))
            st = _dot_nt(k2, jnp.concatenate(qm, axis=0))
            pt = []
            for hh in range(group):
                sink = sinks_ref[j * group + hh]
                sc = jnp.where(mask, st[:, hh * CHUNK:(hh + 1) * CHUNK], NEG)
                m = jnp.maximum(jnp.max(sc, axis=0, keepdims=True), sink)
                e = jnp.exp(sc - m)
                denom = jnp.sum(e, axis=0, keepdims=True) + jnp.exp(sink - m)
                pt.append((e * (1.0 / denom)).astype(BF16))
            ot = _dot(v2t, jnp.concatenate(pt, axis=1))
            for pair in range(group // 2):
                even = ot[:, (2 * pair) * CHUNK:(2 * pair + 1) * CHUNK]
                odd = ot[:, (2 * pair + 1) * CHUNK:(2 * pair + 2) * CHUNK]
                ybt.append(jnp.where(top_half, even, odd))
        ybt = jnp.concatenate(ybt, axis=0)
        ybt = ybt * lax.rsqrt(jnp.mean(ybt * ybt, axis=0, keepdims=True) + EPS)
        y_ref[blk, d_a:d_a + d_b] = (ybt.T * gb_ref[...]).astype(BF16)
        return carry

    lax.fori_loop(0, n_blocks, block, 0)

    o_ref[...] = h_ref[...] + _dot(y_ref[...], wout_ref[...])


def _mixer(sinks, u, v, q, kv, h, g_v, w_s, b_s_t, g_a, g_b, w_out, *, tm=512):
    s, d = h.shape
    d_a = u.shape[1]
    d_b = q.shape[1]
    d_kv = kv.shape[1]
    d_mix = d_a + d_b
    n_blocks = tm // CHUNK
    limit = _vmem_limit(
        2 * 2 * _nbytes((tm, d_a), BF16), 2 * _nbytes((tm, d_b), BF16),
        2 * _nbytes((tm + CHUNK, d_kv), BF16), 4 * _nbytes((tm, d), F32),
        _nbytes((d_mix, d), BF16), _nbytes((tm, d_mix), BF16), _nbytes((tm + CHUNK, d_kv), BF16),
        2 * _nbytes((A_GROUPS, CHUNK, CHUNK), F32), 2 * _nbytes((tm, d), F32),
        16 * _nbytes((CHUNK, d_a), F32))
    vec = lambda n: pl.BlockSpec((1, n), lambda i: (0, 0))
    return pl.pallas_call(
        _mixer_kernel,
        grid=(s // tm,),
        in_specs=[
            pl.BlockSpec(memory_space=pltpu.SMEM),
            pl.BlockSpec((tm, d_a), lambda i: (i, 0)),
            pl.BlockSpec((tm, d_a), lambda i: (i, 0)),
            pl.BlockSpec((tm, d_b), lambda i: (i, 0)),
            pl.BlockSpec((tm, d_kv), lambda i: (i, 0)),
            pl.BlockSpec((CHUNK, d_kv), lambda i: (jnp.maximum(i * n_blocks - 1, 0), 0)),
            pl.BlockSpec((tm, d), lambda i: (i, 0)),
            vec(d_a),
            pl.BlockSpec((A_GROUPS, CHUNK, CHUNK), lambda i: (0, 0, 0)),
            pl.BlockSpec((CHUNK, A_GROUPS), lambda i: (0, 0)),
            vec(d_a),
            vec(d_b),
            pl.BlockSpec((d_mix, d), lambda i: (0, 0), pipeline_mode=pl.Buffered(1)),
        ],
        out_specs=pl.BlockSpec((tm, d), lambda i: (i, 0)),
        out_shape=jax.ShapeDtypeStruct((s, d), F32),
        scratch_shapes=[
            pltpu.VMEM((tm + CHUNK, d_kv), BF16),
            pltpu.VMEM((A_GROUPS, CHUNK, CHUNK), BF16),
            pltpu.VMEM((tm, d_mix), BF16),
        ],
        compiler_params=pltpu.CompilerParams(dimension_semantics=_ARB1, vmem_limit_bytes=limit),
        name="mixer",
    )(sinks, u, v, q, kv, kv, h, g_v, w_s, b_s_t, g_a, g_b, w_out)


def _mem_kv_kernel(mem_ref, g_ref, w_ref, kv_ref):
    mn = _rmsnorm(mem_ref[...], g_ref[...]).astype(BF16)
    kv_ref[...] = _dot(mn, w_ref[...].astype(BF16)).astype(BF16)


def _mem_kv(mem, g, w, *, tn=512):
    m, d = mem.shape
    n = w.shape[1]
    limit = _vmem_limit(
        2 * _nbytes((m, d), F32), 2 * _nbytes((d, tn), F32), _nbytes((d, tn), BF16),
        2 * _nbytes((m, tn), BF16), 3 * _nbytes((m, d), F32), 2 * _nbytes((m, tn), F32))
    return pl.pallas_call(
        _mem_kv_kernel,
        grid=(n // tn,),
        in_specs=[
            pl.BlockSpec((m, d), lambda j: (0, 0)),
            pl.BlockSpec((1, d), lambda j: (0, 0)),
            pl.BlockSpec((d, tn), lambda j: (0, j)),
        ],
        out_specs=pl.BlockSpec((m, tn), lambda j: (0, j)),
        out_shape=jax.ShapeDtypeStruct((m, n), BF16),
        compiler_params=pltpu.CompilerParams(dimension_semantics=_ARB1, vmem_limit_bytes=limit),
        name="mem_kv",
    )(mem, g, w)


def _cross_kernel(h_ref, g_ref, wq_ref, kv_ref, wo_ref, o_ref):
    x = h_ref[...]
    d = x.shape[1]
    hd = d // X_HEADS
    hn = _rmsnorm(x, g_ref[...]).astype(BF16)
    outs = []
    for head in range(X_HEADS):
        cols = slice(head * hd, (head + 1) * hd)
        q = _dot(hn, wq_ref[:, cols]).astype(BF16)
        sc = _dot_nt(q, kv_ref[:, cols]) * (hd ** -0.5)
        m = jnp.max(sc, axis=1, keepdims=True)
        e = jnp.exp(sc - m)
        p = (e * (1.0 / jnp.sum(e, axis=1, keepdims=True))).astype(BF16)
        outs.append(_dot(p, kv_ref[:, d + head * hd:d + (head + 1) * hd]).astype(BF16))
    o_ref[...] = x + _dot(jnp.concatenate(outs, axis=1), wo_ref[...])


def _cross(h, g, wq, kv, wo, *, tm=512):
    s, d = h.shape
    m = kv.shape[0]
    limit = _vmem_limit(
        4 * _nbytes((tm, d), F32), 2 * _nbytes((d, d), BF16), 2 * _nbytes((m, 2 * d), BF16),
        3 * _nbytes((tm, d), F32), 2 * _nbytes((tm, d), BF16), 4 * _nbytes((tm, d // X_HEADS), F32))
    const = lambda shape: pl.BlockSpec(shape, lambda i: (0, 0), pipeline_mode=pl.Buffered(1))
    return pl.pallas_call(
        _cross_kernel,
        grid=(s // tm,),
        in_specs=[
            pl.BlockSpec((tm, d), lambda i: (i, 0)),
            pl.BlockSpec((1, d), lambda i: (0, 0)),
            const((d, d)),
            pl.BlockSpec((m, 2 * d), lambda i: (0, 0)),
            const((d, d)),
        ],
        out_specs=pl.BlockSpec((tm, d), lambda i: (i, 0)),
        out_shape=jax.ShapeDtypeStruct((s, d), F32),
        compiler_params=pltpu.CompilerParams(dimension_semantics=_ARB1, vmem_limit_bytes=limit),
        name="cross_attn",
    )(h, g, wq, kv, wo)


def _final_norm_kernel(h_ref, g_ref, o_ref):
    o_ref[...] = _rmsnorm(h_ref[...], g_ref[...])


def _final_norm(h, g, *, tm=1024):
    s, d = h.shape
    limit = _vmem_limit(4 * _nbytes((tm, d), F32), 2 * _nbytes((tm, d), F32))
    return pl.pallas_call(
        _final_norm_kernel,
        grid=(s // tm,),
        in_specs=[pl.BlockSpec((tm, d), lambda i: (i, 0)), pl.BlockSpec((1, d), lambda i: (0, 0))],
        out_specs=pl.BlockSpec((tm, d), lambda i: (i, 0)),
        out_shape=jax.ShapeDtypeStruct((s, d), F32),
        compiler_params=pltpu.CompilerParams(dimension_semantics=_ARB1, vmem_limit_bytes=limit),
        name="final_norm",
    )(h, g)


def _row(v):
    return v.reshape(1, -1)


def kernel(x, mem, g_ffn1, w1_gate, w1_up, w1_down, g_mix, w_in, g_v, w_s, b_s, sinks, g_a_out,
           g_b_out, w_out, g_x, g_mem, w_xq, w_xkv, w_xo, g_ffn2, w2_gate, w2_up, w2_down, g_final):
    depth = g_ffn1.shape[0]
    d_a = g_v.shape[1]
    d_b = g_b_out.shape[1]
    d_kv = 2 * B_KV_HEADS * HEAD_DIM
    outs = []
    for b in range(x.shape[0]):
        h = x[b]
        for l in range(depth):
            act = _ffn_up(h, _row(g_ffn1[l]), w1_gate[l].astype(BF16), w1_up[l].astype(BF16))
            h = _ffn_down(act, w1_down[l].astype(BF16), h)

            u, v, q, kv = _in_proj(h, _row(g_mix[l]), w_in[l].astype(BF16), d_a, d_b, d_kv)
            h = _mixer(sinks[l], u, v, q, kv, h, _row(g_v[l]), w_s[l], jnp.transpose(b_s[l]),
                       _row(g_a_out[l]), _row(g_b_out[l]), w_out[l].astype(BF16))

            mem_kv = _mem_kv(mem[b], _row(g_mem[l]), w_xkv[l])
            h = _cross(h, _row(g_x[l]), w_xq[l].astype(BF16), mem_kv, w_xo[l].astype(BF16))

            act = _ffn_up(h, _row(g_ffn2[l]), w2_gate[l].astype(BF16), w2_up[l].astype(BF16))
            h = _ffn_down(act, w2_down[l].astype(BF16), h)
        outs.append(_final_norm(h, _row(g_final)))
    return outs[0][None] if len(outs) == 1 else jnp.stack(outs, axis=0)
```

```python
import functools

import jax
import jax.numpy as jnp
from jax import lax
from jax.experimental import pallas as pl
from jax.experimental.pallas import tpu as pltpu

F32 = jnp.float32
BF16 = jnp.bfloat16

EPS = 1e-5
NEG = -1e30
CHUNK = 128
A_GROUPS = 8
HEAD_DIM = 64
B_KV_HEADS = 2
X_HEADS = 4

V7X_VMEM_BYTES = 64 * 2**20
V7X_LANES = 128
V7X_MXU_N = 256


def _vmem_limit(*byte_counts):
    return int(min(sum(byte_counts), V7X_VMEM_BYTES))


def _nbytes(shape, dtype):
    n = 1
    for s in shape:
        n *= s
    return n * jnp.dtype(dtype).itemsize


def _rmsnorm(x, g):
    ms = jnp.mean(x * x, axis=-1, keepdims=True)
    return x * lax.rsqrt(ms + EPS) * g


def _gelu(x):
    return 0.5 * x * (1.0 + lax.erf(x * (2.0 ** -0.5)))


def _dot(a, b):
    return jnp.dot(a, b, preferred_element_type=F32)


def _dot_nt(a, b):
    return lax.dot_general(a, b, (((1,), (1,)), ((), ())), preferred_element_type=F32)


_ARB1 = ("arbitrary",)
_ARB2 = ("arbitrary", "arbitrary")


def _ffn_kernel(h_ref, g_ref, wg_ref, wu_ref, wd_ref, gf_ref, o_ref, hn_ref, *, down_n, final_norm):
    j = pl.program_id(1)

    @pl.when(j == 0)
    def _():
        x = h_ref[...]
        hn_ref[...] = _rmsnorm(x, g_ref[...]).astype(BF16)
        o_ref[...] = x

    hn = hn_ref[...]
    tf = wg_ref.shape[1]
    act = []
    for c in range(0, tf, V7X_MXU_N):
        cols = slice(c, c + V7X_MXU_N)
        gate = _dot(hn, wg_ref[:, cols].astype(BF16))
        up = _dot(hn, wu_ref[:, cols].astype(BF16))
        act.append(((0.5 * jax.nn.silu(gate)) * up).astype(BF16))
    act = act[0] if len(act) == 1 else jnp.concatenate(act, axis=1)
    for n in range(0, o_ref.shape[1], down_n):
        cols = slice(n, n + down_n)
        o_ref[:, cols] += _dot(act, wd_ref[:, cols].astype(BF16))

    if final_norm:
        @pl.when(j == pl.num_programs(1) - 1)
        def _():
            o_ref[...] = _rmsnorm(o_ref[...], gf_ref[...])


def _ffn(h, g, wg, wu, wd, g_final, *, final_norm, tm=1024, tf=512, down_n=512):
    s, d = h.shape
    f = wg.shape[1]
    limit = _vmem_limit(
        _nbytes((tm, d), F32), 2 * _nbytes((tm, d), F32), _nbytes((tm, d), BF16),
        2 * 3 * _nbytes((d, tf), F32), 3 * _nbytes((d, tf), BF16),
        4 * _nbytes((tm, V7X_MXU_N), F32), _nbytes((tm, tf), BF16),
        2 * _nbytes((tm, down_n), F32), _nbytes((tm, d), F32))
    return pl.pallas_call(
        functools.partial(_ffn_kernel, down_n=down_n, final_norm=final_norm),
        grid=(s // tm, f // tf),
        in_specs=[
            pl.BlockSpec((tm, d), lambda i, j: (i, 0), pipeline_mode=pl.Buffered(1)),
            pl.BlockSpec((1, d), lambda i, j: (0, 0)),
            pl.BlockSpec((d, tf), lambda i, j: (0, j)),
            pl.BlockSpec((d, tf), lambda i, j: (0, j)),
            pl.BlockSpec((tf, d), lambda i, j: (j, 0)),
            pl.BlockSpec((1, d), lambda i, j: (0, 0)),
        ],
        out_specs=pl.BlockSpec((tm, d), lambda i, j: (i, 0)),
        out_shape=jax.ShapeDtypeStruct((s, d), F32),
        scratch_shapes=[pltpu.VMEM((tm, d), BF16)],
        compiler_params=pltpu.CompilerParams(dimension_semantics=_ARB2, vmem_limit_bytes=limit),
        name="ffn",
    )(h, g, wg, wu, wd, g_final)


def _in_proj_kernel(h_ref, g_ref, w_ref, u_ref, v_ref, q_ref, kv_ref):
    hn = _rmsnorm(h_ref[...], g_ref[...]).astype(BF16)
    d_a = u_ref.shape[1]
    d_b = q_ref.shape[1]

    def proj(lo):
        return _dot(hn, w_ref[:, lo:lo + V7X_MXU_N])

    for c in range(0, d_a, V7X_MXU_N):
        u_ref[:, c:c + V7X_MXU_N] = _gelu(proj(c)).astype(BF16)
    for c in range(0, d_a, V7X_MXU_N):
        v_ref[:, c:c + V7X_MXU_N] = _gelu(proj(d_a + c)).astype(BF16)
    for c in range(0, d_b, V7X_MXU_N):
        q_ref[:, c:c + V7X_MXU_N] = proj(2 * d_a + c).astype(BF16)
    for c in range(0, kv_ref.shape[1], V7X_MXU_N):
        kv_ref[:, c:c + V7X_MXU_N] = proj(2 * d_a + d_b + c).astype(BF16)


def _in_proj(h, g, w, d_a, d_b, d_kv, *, tm=1024):
    s, d = h.shape
    n = w.shape[1]
    limit = _vmem_limit(
        2 * _nbytes((tm, d), F32), _nbytes((d, n), BF16), 2 * _nbytes((tm, n), BF16),
        _nbytes((tm, d), F32), _nbytes((tm, d), BF16), 6 * _nbytes((tm, V7X_MXU_N), F32))
    return pl.pallas_call(
        _in_proj_kernel,
        grid=(s // tm,),
        in_specs=[
            pl.BlockSpec((tm, d), lambda i: (i, 0)),
            pl.BlockSpec((1, d), lambda i: (0, 0)),
            pl.BlockSpec((d, n), lambda i: (0, 0), pipeline_mode=pl.Buffered(1)),
        ],
        out_specs=[
            pl.BlockSpec((tm, d_a), lambda i: (i, 0)),
            pl.BlockSpec((tm, d_a), lambda i: (i, 0)),
            pl.BlockSpec((tm, d_b), lambda i: (i, 0)),
            pl.BlockSpec((tm, d_kv), lambda i: (i, 0)),
        ],
        out_shape=[
            jax.ShapeDtypeStruct((s, d_a), BF16),
            jax.ShapeDtypeStruct((s, d_a), BF16),
            jax.ShapeDtypeStruct((s, d_b), BF16),
            jax.ShapeDtypeStruct((s, d_kv), BF16),
        ],
        compiler_params=pltpu.CompilerParams(dimension_semantics=_ARB1, vmem_limit_bytes=limit),
        name="in_proj",
    )(h, g, w)


def _mixer_kernel(sinks_ref, u_ref, v_ref, q_ref, kvc_ref, kvp_ref, h_ref, gv_ref, ws_ref, bst_ref,
                  ga_ref, gb_ref, wout_ref, o_ref, kvbuf_ref, wsm_ref, y_ref):
    tm, d_a = u_ref.shape
    d_b = q_ref.shape[1]
    n_blocks = tm // CHUNK
    group = d_b // HEAD_DIM // B_KV_HEADS
    kv_w = B_KV_HEADS * HEAD_DIM
    tile = pl.program_id(0)

    kvbuf_ref[0:CHUNK, :] = kvp_ref[...]
    kvbuf_ref[CHUNK:, :] = kvc_ref[...]

    t_idx = lax.broadcasted_iota(jnp.int32, (CHUNK, CHUNK), 0)
    s_idx = lax.broadcasted_iota(jnp.int32, (CHUNK, CHUNK), 1)
    for g in range(A_GROUPS):
        wsm_ref[g] = jnp.where(s_idx <= t_idx, ws_ref[g], 0.0).astype(BF16)

    low_half = lax.broadcasted_iota(jnp.int32, (CHUNK, V7X_LANES), 1) < HEAD_DIM
    low2 = lax.broadcasted_iota(jnp.int32, (2 * CHUNK, V7X_LANES), 1) < HEAD_DIM
    top_half = lax.broadcasted_iota(jnp.int32, (V7X_LANES, CHUNK), 0) < HEAD_DIM
    k_pos = lax.broadcasted_iota(jnp.int32, (2 * CHUNK, CHUNK), 0)
    q_pos = lax.broadcasted_iota(jnp.int32, (2 * CHUNK, CHUNK), 1)
    band = (k_pos > q_pos) & (k_pos <= q_pos + CHUNK)

    def block(b, carry):
        r0 = pl.multiple_of(b * CHUNK, CHUNK)
        blk = pl.ds(r0, CHUNK)

        vn = _rmsnorm(v_ref[blk, :].astype(F32), gv_ref[...]).astype(BF16)
        ub = u_ref[blk, :].astype(F32)
        ya = []
        for g in range(A_GROUPS):
            cols = slice(g * CHUNK, (g + 1) * CHUNK)
            sv = _dot(wsm_ref[g], vn[:, cols]) + bst_ref[:, g:g + 1]
            ya.append(ub[:, cols] * sv)
        ya = jnp.concatenate(ya, axis=1)
        y_ref[blk, 0:d_a] = _rmsnorm(ya, ga_ref[...]).astype(BF16)

        qb = q_ref[blk, :] * (HEAD_DIM ** -0.5)
        kk = kvbuf_ref[pl.ds(r0, 2 * CHUNK), :]
        k_f = kk[:, 0:kv_w].astype(F32)
        v_f = kk[:, kv_w:2 * kv_w].astype(F32)
        k_rot = pltpu.roll(k_f, HEAD_DIM, axis=1)
        v_rot = pltpu.roll(v_f, HEAD_DIM, axis=1)
        first = jnp.logical_and(tile == 0, b == 0)
        mask = band & (k_pos >= jnp.where(first, CHUNK, 0))
        ybt = []
        for j in range(B_KV_HEADS):
            if j == 0:
                k2 = jnp.where(low2, k_f, k_rot).astype(BF16)
                v2t = jnp.where(low2, v_f, v_rot).T.astype(BF16)
            else:
                k2 = jnp.where(low2, k_rot, k_f).astype(BF16)
                v2t = jnp.where(low2, v_rot, v_f).T.astype(BF16)
            qm = []
            for hh in range(group):
                head = j * group + hh
                c0 = (head // 2) * V7X_LANES
                qc = qb[:, c0:c0 + V7X_LANES]
                keep = low_half if head % 2 == 0 else jnp.logical_not(low_half)
                qm.append(jnp.where(keep, qc, jnp.zeros_like(qc)))
            st = _dot_nt(k2, jnp.concatenate(qm, axis=0))
            pt = []
            for hh in range(group):
                sink = sinks_ref[j * group + hh]
                sc = jnp.where(mask, st[:, hh * CHUNK:(hh + 1) * CHUNK], NEG)
                m = jnp.maximum(jnp.max(sc, axis=0, keepdims=True), sink)
                e = jnp.exp(sc - m)
                denom = jnp.sum(e, axis=0, keepdims=True) + jnp.exp(sink - m)
                pt.append((e * (1.0 / denom)).astype(BF16))
            ot = _dot(v2t, jnp.concatenate(pt, axis=1))
            for pair in range(group // 2):
                even = ot[:, (2 * pair) * CHUNK:(2 * pair + 1) * CHUNK]
                odd = ot[:, (2 * pair + 1) * CHUNK:(2 * pair + 2) * CHUNK]
                ybt.append(jnp.where(top_half, even, odd))
        ybt = jnp.concatenate(ybt, axis=0)
        ybt = ybt * lax.rsqrt(jnp.mean(ybt * ybt, axis=0, keepdims=True) + EPS)
        y_ref[blk, d_a:d_a + d_b] = (ybt.T * gb_ref[...]).astype(BF16)
        return carry

    lax.fori_loop(0, n_blocks, block, 0)

    o_ref[...] = h_ref[...] + _dot(y_ref[...], wout_ref[...])


def _mixer(sinks, u, v, q, kv, h, g_v, w_s, b_s_t, g_a, g_b, w_out, *, tm=512):
    s, d = h.shape
    d_a = u.shape[1]
    d_b = q.shape[1]
    d_kv = kv.shape[1]
    d_mix = d_a + d_b
    n_blocks = tm // CHUNK
    limit = _vmem_limit(
        2 * 2 * _nbytes((tm, d_a), BF16), 2 * _nbytes((tm, d_b), BF16),
        2 * _nbytes((tm + CHUNK, d_kv), BF16), 4 * _nbytes((tm, d), F32),
        _nbytes((d_mix, d), BF16), _nbytes((tm, d_mix), BF16), _nbytes((tm + CHUNK, d_kv), BF16),
        2 * _nbytes((A_GROUPS, CHUNK, CHUNK), F32), 2 * _nbytes((tm, d), F32),
        16 * _nbytes((CHUNK, d_a), F32))
    vec = lambda n: pl.BlockSpec((1, n), lambda i: (0, 0))
    return pl.pallas_call(
        _mixer_kernel,
        grid=(s // tm,),
        in_specs=[
            pl.BlockSpec(memory_space=pltpu.SMEM),
            pl.BlockSpec((tm, d_a), lambda i: (i, 0)),
            pl.BlockSpec((tm, d_a), lambda i: (i, 0)),
            pl.BlockSpec((tm, d_b), lambda i: (i, 0)),
            pl.BlockSpec((tm, d_kv), lambda i: (i, 0)),
            pl.BlockSpec((CHUNK, d_kv), lambda i: (jnp.maximum(i * n_blocks - 1, 0), 0)),
            pl.BlockSpec((tm, d), lambda i: (i, 0)),
            vec(d_a),
            pl.BlockSpec((A_GROUPS, CHUNK, CHUNK), lambda i: (0, 0, 0)),
            pl.BlockSpec((CHUNK, A_GROUPS), lambda i: (0, 0)),
            vec(d_a),
            vec(d_b),
            pl.BlockSpec((d_mix, d), lambda i: (0, 0), pipeline_mode=pl.Buffered(1)),
        ],
        out_specs=pl.BlockSpec((tm, d), lambda i: (i, 0)),
        out_shape=jax.ShapeDtypeStruct((s, d), F32),
        scratch_shapes=[
            pltpu.VMEM((tm + CHUNK, d_kv), BF16),
            pltpu.VMEM((A_GROUPS, CHUNK, CHUNK), BF16),
            pltpu.VMEM((tm, d_mix), BF16),
        ],
        compiler_params=pltpu.CompilerParams(dimension_semantics=_ARB1, vmem_limit_bytes=limit),
        name="mixer",
    )(sinks, u, v, q, kv, kv, h, g_v, w_s, b_s_t, g_a, g_b, w_out)


def _mem_kv_kernel(mem_ref, g_ref, w_ref, kv_ref):
    mn = _rmsnorm(mem_ref[...], g_ref[...]).astype(BF16)
    kv_ref[...] = _dot(mn, w_ref[...].astype(BF16)).astype(BF16)


def _mem_kv(mem, g, w, *, tn=512):
    m, d = mem.shape
    n = w.shape[1]
    limit = _vmem_limit(
        2 * _nbytes((m, d), F32), 2 * _nbytes((d, tn), F32), _nbytes((d, tn), BF16),
        2 * _nbytes((m, tn), BF16), 3 * _nbytes((m, d), F32), 2 * _nbytes((m, tn), F32))
    return pl.pallas_call(
        _mem_kv_kernel,
        grid=(n // tn,),
        in_specs=[
            pl.BlockSpec((m, d), lambda j: (0, 0)),
            pl.BlockSpec((1, d), lambda j: (0, 0)),
            pl.BlockSpec((d, tn), lambda j: (0, j)),
        ],
        out_specs=pl.BlockSpec((m, tn), lambda j: (0, j)),
        out_shape=jax.ShapeDtypeStruct((m, n), BF16),
        compiler_params=pltpu.CompilerParams(dimension_semantics=_ARB1, vmem_limit_bytes=limit),
        name="mem_kv",
    )(mem, g, w)


def _cross_kernel(h_ref, g_ref, wq_ref, kv_ref, wo_ref, o_ref):
    x = h_ref[...]
    d = x.shape[1]
    hd = d // X_HEADS
    hn = _rmsnorm(x, g_ref[...]).astype(BF16)
    outs = []
    for head in range(X_HEADS):
        cols = slice(head * hd, (head + 1) * hd)
        q = _dot(hn, wq_ref[:, cols]).astype(BF16)
        sc = _dot_nt(q, kv_ref[:, cols]) * (hd ** -0.5)
        m = jnp.max(sc, axis=1, keepdims=True)
        e = jnp.exp(sc - m)
        p = (e * (1.0 / jnp.sum(e, axis=1, keepdims=True))).astype(BF16)
        outs.append(_dot(p, kv_ref[:, d + head * hd:d + (head + 1) * hd]).astype(BF16))
    o_ref[...] = x + _dot(jnp.concatenate(outs, axis=1), wo_ref[...])


def _cross(h, g, wq, kv, wo, *, tm=512):
    s, d = h.shape
    m = kv.shape[0]
    limit = _vmem_limit(
        4 * _nbytes((tm, d), F32), 2 * _nbytes((d, d), BF16), 2 * _nbytes((m, 2 * d), BF16),
        3 * _nbytes((tm, d), F32), 2 * _nbytes((tm, d), BF16), 4 * _nbytes((tm, d // X_HEADS), F32))
    const = lambda shape: pl.BlockSpec(shape, lambda i: (0, 0), pipeline_mode=pl.Buffered(1))
    return pl.pallas_call(
        _cross_kernel,
        grid=(s // tm,),
        in_specs=[
            pl.BlockSpec((tm, d), lambda i: (i, 0)),
            pl.BlockSpec((1, d), lambda i: (0, 0)),
            const((d, d)),
            pl.BlockSpec((m, 2 * d), lambda i: (0, 0)),
            const((d, d)),
        ],
        out_specs=pl.BlockSpec((tm, d), lambda i: (i, 0)),
        out_shape=jax.ShapeDtypeStruct((s, d), F32),
        compiler_params=pltpu.CompilerParams(dimension_semantics=_ARB1, vmem_limit_bytes=limit),
        name="cross_attn",
    )(h, g, wq, kv, wo)


def _row(v):
    return v.reshape(1, -1)


def kernel(x, mem, g_ffn1, w1_gate, w1_up, w1_down, g_mix, w_in, g_v, w_s, b_s, sinks, g_a_out,
           g_b_out, w_out, g_x, g_mem, w_xq, w_xkv, w_xo, g_ffn2, w2_gate, w2_up, w2_down, g_final):
    depth = g_ffn1.shape[0]
    d_a = g_v.shape[1]
    d_b = g_b_out.shape[1]
    d_kv = 2 * B_KV_HEADS * HEAD_DIM
    outs = []
    for b in range(x.shape[0]):
        h = x[b]
        for l in range(depth):
            h = _ffn(h, _row(g_ffn1[l]), w1_gate[l], w1_up[l], w1_down[l], _row(g_final),
                     final_norm=False)

            u, v, q, kv = _in_proj(h, _row(g_mix[l]), w_in[l].astype(BF16), d_a, d_b, d_kv)
            h = _mixer(sinks[l], u, v, q, kv, h, _row(g_v[l]), w_s[l], jnp.transpose(b_s[l]),
                       _row(g_a_out[l]), _row(g_b_out[l]), w_out[l].astype(BF16))

            mem_kv = _mem_kv(mem[b], _row(g_mem[l]), w_xkv[l])
            h = _cross(h, _row(g_x[l]), w_xq[l].astype(BF16), mem_kv, w_xo[l].astype(BF16))

            h = _ffn(h, _row(g_ffn2[l]), w2_gate[l], w2_up[l], w2_down[l], _row(g_final),
                     final_norm=(l == depth - 1))
        outs.append(h)
    return outs[0][None] if len(outs) == 1 else jnp.stack(outs, axis=0)
```

```python
import functools

import jax
import jax.numpy as jnp
from jax import lax
from jax.experimental import pallas as pl
from jax.experimental.pallas import tpu as pltpu

F32 = jnp.float32
BF16 = jnp.bfloat16

EPS = 1e-5
NEG = -1e30
CHUNK = 128
A_GROUPS = 8
HEAD_DIM = 64
B_KV_HEADS = 2
X_HEADS = 4

V7X_VMEM_BYTES = 64 * 2**20
V7X_LANES = 128
V7X_MXU_N = 256
PROJ_M = 256
PROJ_N = 256


def _vmem_limit(*byte_counts):
    return int(min(sum(byte_counts), V7X_VMEM_BYTES))


def _nbytes(shape, dtype):
    n = 1
    for s in shape:
        n *= s
    return n * jnp.dtype(dtype).itemsize


def _rmsnorm(x, g):
    ms = jnp.mean(x * x, axis=-1, keepdims=True)
    return x * lax.rsqrt(ms + EPS) * g


def _gelu(x):
    return 0.5 * x * (1.0 + lax.erf(x * (2.0 ** -0.5)))


def _dot(a, b):
    return jnp.dot(a, b, preferred_element_type=F32)


def _dot_nt(a, b):
    return lax.dot_general(a, b, (((1,), (1,)), ((), ())), preferred_element_type=F32)


_ARB1 = ("arbitrary",)
_ARB2 = ("arbitrary", "arbitrary")


def _ffn_kernel(h_ref, g_ref, wg_ref, wu_ref, wd_ref, gf_ref, o_ref, hn_ref, *, down_n, final_norm):
    j = pl.program_id(1)

    @pl.when(j == 0)
    def _():
        x = h_ref[...]
        hn_ref[...] = _rmsnorm(x, g_ref[...]).astype(BF16)
        o_ref[...] = x

    hn = hn_ref[...]
    tf = wg_ref.shape[1]
    act = []
    for c in range(0, tf, V7X_MXU_N):
        cols = slice(c, c + V7X_MXU_N)
        gate = _dot(hn, wg_ref[:, cols].astype(BF16))
        up = _dot(hn, wu_ref[:, cols].astype(BF16))
        act.append(((0.5 * jax.nn.silu(gate)) * up).astype(BF16))
    act = act[0] if len(act) == 1 else jnp.concatenate(act, axis=1)
    for n in range(0, o_ref.shape[1], down_n):
        cols = slice(n, n + down_n)
        o_ref[:, cols] += _dot(act, wd_ref[:, cols].astype(BF16))

    if final_norm:
        @pl.when(j == pl.num_programs(1) - 1)
        def _():
            o_ref[...] = _rmsnorm(o_ref[...], gf_ref[...])


def _ffn(h, g, wg, wu, wd, g_final, *, final_norm, tm=1024, tf=512, down_n=512):
    s, d = h.shape
    f = wg.shape[1]
    limit = _vmem_limit(
        _nbytes((tm, d), F32), 2 * _nbytes((tm, d), F32), _nbytes((tm, d), BF16),
        2 * 3 * _nbytes((d, tf), F32), 3 * _nbytes((d, tf), BF16),
        4 * _nbytes((tm, V7X_MXU_N), F32), _nbytes((tm, tf), BF16),
        2 * _nbytes((tm, down_n), F32), _nbytes((tm, d), F32))
    return pl.pallas_call(
        functools.partial(_ffn_kernel, down_n=down_n, final_norm=final_norm),
        grid=(s // tm, f // tf),
        in_specs=[
            pl.BlockSpec((tm, d), lambda i, j: (i, 0), pipeline_mode=pl.Buffered(1)),
            pl.BlockSpec((1, d), lambda i, j: (0, 0)),
            pl.BlockSpec((d, tf), lambda i, j: (0, j)),
            pl.BlockSpec((d, tf), lambda i, j: (0, j)),
            pl.BlockSpec((tf, d), lambda i, j: (j, 0)),
            pl.BlockSpec((1, d), lambda i, j: (0, 0)),
        ],
        out_specs=pl.BlockSpec((tm, d), lambda i, j: (i, 0)),
        out_shape=jax.ShapeDtypeStruct((s, d), F32),
        scratch_shapes=[pltpu.VMEM((tm, d), BF16)],
        compiler_params=pltpu.CompilerParams(dimension_semantics=_ARB2, vmem_limit_bytes=limit),
        name="ffn",
    )(h, g, wg, wu, wd, g_final)


def _in_proj_kernel(h_ref, g_ref, w_ref, u_ref, v_ref, q_ref, kv_ref):
    hn = _rmsnorm(h_ref[...], g_ref[...]).astype(BF16)
    d_a = u_ref.shape[1]
    d_b = q_ref.shape[1]

    def proj(lo):
        return _dot(hn, w_ref[:, lo:lo + V7X_MXU_N])

    for c in range(0, d_a, V7X_MXU_N):
        u_ref[:, c:c + V7X_MXU_N] = _gelu(proj(c)).astype(BF16)
    for c in range(0, d_a, V7X_MXU_N):
        v_ref[:, c:c + V7X_MXU_N] = _gelu(proj(d_a + c)).astype(BF16)
    for c in range(0, d_b, V7X_MXU_N):
        q_ref[:, c:c + V7X_MXU_N] = proj(2 * d_a + c).astype(BF16)
    for c in range(0, kv_ref.shape[1], V7X_MXU_N):
        kv_ref[:, c:c + V7X_MXU_N] = proj(2 * d_a + d_b + c).astype(BF16)


def _in_proj(h, g, w, d_a, d_b, d_kv, *, tm=1024):
    s, d = h.shape
    n = w.shape[1]
    limit = _vmem_limit(
        2 * _nbytes((tm, d), F32), _nbytes((d, n), BF16), 2 * _nbytes((tm, n), BF16),
        _nbytes((tm, d), F32), _nbytes((tm, d), BF16), 6 * _nbytes((tm, V7X_MXU_N), F32))
    return pl.pallas_call(
        _in_proj_kernel,
        grid=(s // tm,),
        in_specs=[
            pl.BlockSpec((tm, d), lambda i: (i, 0)),
            pl.BlockSpec((1, d), lambda i: (0, 0)),
            pl.BlockSpec((d, n), lambda i: (0, 0), pipeline_mode=pl.Buffered(1)),
        ],
        out_specs=[
            pl.BlockSpec((tm, d_a), lambda i: (i, 0)),
            pl.BlockSpec((tm, d_a), lambda i: (i, 0)),
            pl.BlockSpec((tm, d_b), lambda i: (i, 0)),
            pl.BlockSpec((tm, d_kv), lambda i: (i, 0)),
        ],
        out_shape=[
            jax.ShapeDtypeStruct((s, d_a), BF16),
            jax.ShapeDtypeStruct((s, d_a), BF16),
            jax.ShapeDtypeStruct((s, d_b), BF16),
            jax.ShapeDtypeStruct((s, d_kv), BF16),
        ],
        compiler_params=pltpu.CompilerParams(dimension_semantics=_ARB1, vmem_limit_bytes=limit),
        name="in_proj",
    )(h, g, w)


def _mixer_kernel(sinks_ref, u_ref, v_ref, q_ref, kvc_ref, kvp_ref, h_ref, gv_ref, ws_ref, bst_ref,
                  ga_ref, gb_ref, wout_ref, o_ref, kvbuf_ref, wsm_ref, y_ref):
    tm, d_a = u_ref.shape
    d_b = q_ref.shape[1]
    n_blocks = tm // CHUNK
    group = d_b // HEAD_DIM // B_KV_HEADS
    kv_w = B_KV_HEADS * HEAD_DIM
    step = pl.program_id(0)
    tile = jnp.minimum(step, pl.num_programs(0) - 2)
    y_new = y_ref.at[step % 2]
    y_old = y_ref.at[(step + 1) % 2]

    @pl.when(step == 0)
    def _():
        y_ref[1] = jnp.zeros(y_ref.shape[1:], BF16)

    kvbuf_ref[0:CHUNK, :] = kvp_ref[...]
    kvbuf_ref[CHUNK:, :] = kvc_ref[...]

    t_idx = lax.broadcasted_iota(jnp.int32, (CHUNK, CHUNK), 0)
    s_idx = lax.broadcasted_iota(jnp.int32, (CHUNK, CHUNK), 1)
    for g in range(A_GROUPS):
        wsm_ref[g] = jnp.where(s_idx <= t_idx, ws_ref[g], 0.0).astype(BF16)

    low_half = lax.broadcasted_iota(jnp.int32, (CHUNK, V7X_LANES), 1) < HEAD_DIM
    low2 = lax.broadcasted_iota(jnp.int32, (2 * CHUNK, V7X_LANES), 1) < HEAD_DIM
    top_half = lax.broadcasted_iota(jnp.int32, (V7X_LANES, CHUNK), 0) < HEAD_DIM
    k_pos = lax.broadcasted_iota(jnp.int32, (2 * CHUNK, CHUNK), 0)
    q_pos = lax.broadcasted_iota(jnp.int32, (2 * CHUNK, CHUNK), 1)
    band = (k_pos > q_pos) & (k_pos <= q_pos + CHUNK)

    pieces = [(r, c) for c in range(0, o_ref.shape[1], PROJ_N) for r in range(0, tm, PROJ_M)]
    n_pieces, n_stages = len(pieces), 4 * n_blocks
    stage = [0]

    def project():
        stage[0] += 1
        while n_pieces - len(pieces) < stage[0] * n_pieces // n_stages:
            r, c = pieces.pop(0)
            rows, cols = slice(r, r + PROJ_M), slice(c, c + PROJ_N)
            o_ref[rows, cols] = h_ref[rows, cols] + _dot(y_old[rows, :], wout_ref[:, cols])

    for b in range(n_blocks):
        r0 = b * CHUNK
        blk = slice(r0, r0 + CHUNK)

        project()
        vn = _rmsnorm(v_ref[blk, :].astype(F32), gv_ref[...]).astype(BF16)
        ub = u_ref[blk, :].astype(F32)
        ya = []
        for g in range(A_GROUPS):
            cols = slice(g * CHUNK, (g + 1) * CHUNK)
            sv = _dot(wsm_ref[g], vn[:, cols]) + bst_ref[:, g:g + 1]
            ya.append(ub[:, cols] * sv)
        ya = jnp.concatenate(ya, axis=1)
        y_new[blk, 0:d_a] = _rmsnorm(ya, ga_ref[...]).astype(BF16)

        qb = q_ref[blk, :] * (HEAD_DIM ** -0.5)
        kk = kvbuf_ref[r0:r0 + 2 * CHUNK, :]
        k_f = kk[:, 0:kv_w].astype(F32)
        v_f = kk[:, kv_w:2 * kv_w].astype(F32)
        k_rot = pltpu.roll(k_f, HEAD_DIM, axis=1)
        v_rot = pltpu.roll(v_f, HEAD_DIM, axis=1)
        mask = band & (k_pos >= jnp.where(tile == 0, CHUNK, 0)) if b == 0 else band
        ybt = []
        for j in range(B_KV_HEADS):
            project()
            if j == 0:
                k2 = jnp.where(low2, k_f, k_rot).astype(BF16)
                v2t = jnp.where(low2, v_f, v_rot).T.astype(BF16)
            else:
                k2 = jnp.where(low2, k_rot, k_f).astype(BF16)
                v2t = jnp.where(low2, v_rot, v_f).T.astype(BF16)
            qm = []
            for hh in range(group):
                head = j * group + hh
                c0 = (head // 2) * V7X_LANES
                qc = qb[:, c0:c0 + V7X_LANES]
                keep = low_half if head % 2 == 0 else jnp.logical_not(low_half)
                qm.append(jnp.where(keep, qc, jnp.zeros_like(qc)))
            st = _dot_nt(k2, jnp.concatenate(qm, axis=0))
            pt = []
            for hh in range(group):
                sink = sinks_ref[j * group + hh]
                sc = jnp.where(mask, st[:, hh * CHUNK:(hh + 1) * CHUNK], NEG)
                m = jnp.maximum(jnp.max(sc, axis=0, keepdims=True), sink)
                e = jnp.exp(sc - m)
                denom = jnp.sum(e, axis=0, keepdims=True) + jnp.exp(sink - m)
                pt.append((e * (1.0 / denom)).astype(BF16))
            ot = _dot(v2t, jnp.concatenate(pt, axis=1))
            for pair in range(group // 2):
                even = ot[:, (2 * pair) * CHUNK:(2 * pair + 1) * CHUNK]
                odd = ot[:, (2 * pair + 1) * CHUNK:(2 * pair + 2) * CHUNK]
                ybt.append(jnp.where(top_half, even, odd))
        project()
        ybt = jnp.concatenate(ybt, axis=0)
        ybt = ybt * lax.rsqrt(jnp.mean(ybt * ybt, axis=0, keepdims=True) + EPS)
        y_new[blk, d_a:d_a + d_b] = (ybt.T * gb_ref[...]).astype(BF16)
    assert not pieces


def _mixer(sinks, u, v, q, kv, h, g_v, w_s, b_s_t, g_a, g_b, w_out, *, tm=512):
    s, d = h.shape
    d_a = u.shape[1]
    d_b = q.shape[1]
    d_kv = kv.shape[1]
    d_mix = d_a + d_b
    n_blocks = tm // CHUNK
    n_tiles = s // tm
    limit = _vmem_limit(
        2 * 2 * _nbytes((tm, d_a), BF16), 2 * _nbytes((tm, d_b), BF16),
        2 * _nbytes((tm + CHUNK, d_kv), BF16), 4 * _nbytes((tm, d), F32),
        _nbytes((d_mix, d), BF16), 2 * _nbytes((tm, d_mix), BF16),
        _nbytes((tm + CHUNK, d_kv), BF16), 2 * _nbytes((A_GROUPS, CHUNK, CHUNK), F32),
        2 * _nbytes((tm, d // n_blocks), F32), 32 * _nbytes((CHUNK, d_a), F32))
    vec = lambda n: pl.BlockSpec((1, n), lambda i: (0, 0))
    mixed = lambda i: (jnp.minimum(i, n_tiles - 1), 0)
    projected = lambda i: (jnp.maximum(i - 1, 0), 0)
    prev_block = lambda i: (jnp.maximum(jnp.minimum(i, n_tiles - 1) * n_blocks - 1, 0), 0)
    return pl.pallas_call(
        _mixer_kernel,
        grid=(n_tiles + 1,),
        in_specs=[
            pl.BlockSpec(memory_space=pltpu.SMEM),
            pl.BlockSpec((tm, d_a), mixed),
            pl.BlockSpec((tm, d_a), mixed),
            pl.BlockSpec((tm, d_b), mixed),
            pl.BlockSpec((tm, d_kv), mixed),
            pl.BlockSpec((CHUNK, d_kv), prev_block),
            pl.BlockSpec((tm, d), projected),
            vec(d_a),
            pl.BlockSpec((A_GROUPS, CHUNK, CHUNK), lambda i: (0, 0, 0)),
            pl.BlockSpec((CHUNK, A_GROUPS), lambda i: (0, 0)),
            vec(d_a),
            vec(d_b),
            pl.BlockSpec((d_mix, d), lambda i: (0, 0), pipeline_mode=pl.Buffered(1)),
        ],
        out_specs=pl.BlockSpec((tm, d), projected),
        out_shape=jax.ShapeDtypeStruct((s, d), F32),
        scratch_shapes=[
            pltpu.VMEM((tm + CHUNK, d_kv), BF16),
            pltpu.VMEM((A_GROUPS, CHUNK, CHUNK), BF16),
            pltpu.VMEM((2, tm, d_mix), BF16),
        ],
        compiler_params=pltpu.CompilerParams(dimension_semantics=_ARB1, vmem_limit_bytes=limit),
        name="mixer",
    )(sinks, u, v, q, kv, kv, h, g_v, w_s, b_s_t, g_a, g_b, w_out)


def _mem_kv_kernel(mem_ref, g_ref, w_ref, kv_ref):
    mn = _rmsnorm(mem_ref[...], g_ref[...]).astype(BF16)
    kv_ref[...] = _dot(mn, w_ref[...].astype(BF16)).astype(BF16)


def _mem_kv(mem, g, w, *, tn=512):
    m, d = mem.shape
    n = w.shape[1]
    limit = _vmem_limit(
        2 * _nbytes((m, d), F32), 2 * _nbytes((d, tn), F32), _nbytes((d, tn), BF16),
        2 * _nbytes((m, tn), BF16), 3 * _nbytes((m, d), F32), 2 * _nbytes((m, tn), F32))
    return pl.pallas_call(
        _mem_kv_kernel,
        grid=(n // tn,),
        in_specs=[
            pl.BlockSpec((m, d), lambda j: (0, 0)),
            pl.BlockSpec((1, d), lambda j: (0, 0)),
            pl.BlockSpec((d, tn), lambda j: (0, j)),
        ],
        out_specs=pl.BlockSpec((m, tn), lambda j: (0, j)),
        out_shape=jax.ShapeDtypeStruct((m, n), BF16),
        compiler_params=pltpu.CompilerParams(dimension_semantics=_ARB1, vmem_limit_bytes=limit),
        name="mem_kv",
    )(mem, g, w)


def _cross_kernel(h_ref, g_ref, wq_ref, kv_ref, wo_ref, o_ref):
    x = h_ref[...]
    d = x.shape[1]
    hd = d // X_HEADS
    hn = _rmsnorm(x, g_ref[...]).astype(BF16)
    outs = []
    for head in range(X_HEADS):
        cols = slice(head * hd, (head + 1) * hd)
        q = _dot(hn, wq_ref[:, cols]).astype(BF16)
        sc = _dot_nt(q, kv_ref[:, cols]) * (hd ** -0.5)
        m = jnp.max(sc, axis=1, keepdims=True)
        e = jnp.exp(sc - m)
        p = (e * (1.0 / jnp.sum(e, axis=1, keepdims=True))).astype(BF16)
        outs.append(_dot(p, kv_ref[:, d + head * hd:d + (head + 1) * hd]).astype(BF16))
    o_ref[...] = x + _dot(jnp.concatenate(outs, axis=1), wo_ref[...])


def _cross(h, g, wq, kv, wo, *, tm=512):
    s, d = h.shape
    m = kv.shape[0]
    limit = _vmem_limit(
        4 * _nbytes((tm, d), F32), 2 * _nbytes((d, d), BF16), 2 * _nbytes((m, 2 * d), BF16),
        3 * _nbytes((tm, d), F32), 2 * _nbytes((tm, d), BF16), 4 * _nbytes((tm, d // X_HEADS), F32))
    const = lambda shape: pl.BlockSpec(shape, lambda i: (0, 0), pipeline_mode=pl.Buffered(1))
    return pl.pallas_call(
        _cross_kernel,
        grid=(s // tm,),
        in_specs=[
            pl.BlockSpec((tm, d), lambda i: (i, 0)),
            pl.BlockSpec((1, d), lambda i: (0, 0)),
            const((d, d)),
            pl.BlockSpec((m, 2 * d), lambda i: (0, 0)),
            const((d, d)),
        ],
        out_specs=pl.BlockSpec((tm, d), lambda i: (i, 0)),
        out_shape=jax.ShapeDtypeStruct((s, d), F32),
        compiler_params=pltpu.CompilerParams(dimension_semantics=_ARB1, vmem_limit_bytes=limit),
        name="cross_attn",
    )(h, g, wq, kv, wo)


def _row(v):
    return v.reshape(1, -1)


def kernel(x, mem, g_ffn1, w1_gate, w1_up, w1_down, g_mix, w_in, g_v, w_s, b_s, sinks, g_a_out,
           g_b_out, w_out, g_x, g_mem, w_xq, w_xkv, w_xo, g_ffn2, w2_gate, w2_up, w2_down, g_final):
    depth = g_ffn1.shape[0]
    d_a = g_v.shape[1]
    d_b = g_b_out.shape[1]
    d_kv = 2 * B_KV_HEADS * HEAD_DIM
    outs = []
    for b in range(x.shape[0]):
        h = x[b]
        for l in range(depth):
            h = _ffn(h, _row(g_ffn1[l]), w1_gate[l], w1_up[l], w1_down[l], _row(g_final),
                     final_norm=False)

            u, v, q, kv = _in_proj(h, _row(g_mix[l]), w_in[l].astype(BF16), d_a, d_b, d_kv)
            h = _mixer(sinks[l], u, v, q, kv, h, _row(g_v[l]), w_s[l], jnp.transpose(b_s[l]),
                       _row(g_a_out[l]), _row(g_b_out[l]), w_out[l].astype(BF16))

            mem_kv = _mem_kv(mem[b], _row(g_mem[l]), w_xkv[l])
            h = _cross(h, _row(g_x[l]), w_xq[l].astype(BF16), mem_kv, w_xo[l].astype(BF16))

            h = _ffn(h, _row(g_ffn2[l]), w2_gate[l], w2_up[l], w2_down[l], _row(g_final),
                     final_norm=(l == depth - 1))
        outs.append(h)
    return outs[0][None] if len(outs) == 1 else jnp.stack(outs, axis=0)
```

```python
import functools

import jax
import jax.numpy as jnp
from jax import lax
from jax.experimental import pallas as pl
from jax.experimental.pallas import tpu as pltpu

F32 = jnp.float32
BF16 = jnp.bfloat16

EPS = 1e-5
NEG = -1e30
CHUNK = 128
A_GROUPS = 8
HEAD_DIM = 64
B_KV_HEADS = 2
X_HEADS = 4

V7X_VMEM_BYTES = 64 * 2**20
V7X_LANES = 128
V7X_MXU_N = 256
CAST_ROWS = 32
PROJ_M = 256
PROJ_N = 256


def _vmem_limit(*byte_counts):
    return int(min(sum(byte_counts), V7X_VMEM_BYTES))


def _nbytes(shape, dtype):
    n = 1
    for s in shape:
        n *= s
    return n * jnp.dtype(dtype).itemsize


def _rmsnorm(x, g):
    ms = jnp.mean(x * x, axis=-1, keepdims=True)
    return x * lax.rsqrt(ms + EPS) * g


def _gelu(x):
    return 0.5 * x * (1.0 + lax.erf(x * (2.0 ** -0.5)))


def _dot(a, b):
    return jnp.dot(a, b, preferred_element_type=F32)


def _dot_nt(a, b):
    return lax.dot_general(a, b, (((1,), (1,)), ((), ())), preferred_element_type=F32)


def _load_resident(pairs):
    copies = [pltpu.make_async_copy(src, dst, sem) for src, dst, sem in pairs]

    @pl.when(pl.program_id(0) == 0)
    def _():
        for cp in copies:
            cp.start()
        for cp in copies:
            cp.wait()


_HBM = pl.BlockSpec(memory_space=pl.ANY)
_ARB1 = ("arbitrary",)
_ARB2 = ("arbitrary", "arbitrary")


def _ffn_kernel(h_hbm, g_ref, wg_ref, wu_ref, wd_ref, gf_ref, *rest, down_n, final_norm, n_casts):
    cast_in, o_ref, cast_out = rest[:n_casts], rest[n_casts], rest[n_casts + 1:2 * n_casts + 1]
    hn_ref, hbuf_ref, h_sem = rest[2 * n_casts + 1:]
    i, j = pl.program_id(0), pl.program_id(1)
    tm = hbuf_ref.shape[0]

    def h_copy(tile):
        rows = pl.ds(pl.multiple_of(tile * tm, tm), tm)
        return pltpu.make_async_copy(h_hbm.at[rows, :], hbuf_ref, h_sem)

    for src, dst in zip(cast_in, cast_out):
        dst[...] = src[...].astype(BF16)

    @pl.when(jnp.logical_and(i == 0, j == 0))
    def _():
        h_copy(0).start()

    @pl.when(j == 0)
    def _():
        h_copy(i).wait()
        x = hbuf_ref[...]
        hn_ref[...] = _rmsnorm(x, g_ref[...]).astype(BF16)
        o_ref[...] = x

    @pl.when(jnp.logical_and(j == 1, i + 1 < pl.num_programs(0)))
    def _():
        h_copy(i + 1).start()

    hn = hn_ref[...]
    tf = wg_ref.shape[1]
    act = []
    for c in range(0, tf, V7X_MXU_N):
        cols = slice(c, c + V7X_MXU_N)
        gate = _dot(hn, wg_ref[:, cols].astype(BF16))
        up = _dot(hn, wu_ref[:, cols].astype(BF16))
        act.append(((0.5 * jax.nn.silu(gate)) * up).astype(BF16))
    act = act[0] if len(act) == 1 else jnp.concatenate(act, axis=1)
    for n in range(0, o_ref.shape[1], down_n):
        cols = slice(n, n + down_n)
        o_ref[:, cols] += _dot(act, wd_ref[:, cols].astype(BF16))

    if final_norm:
        @pl.when(j == pl.num_programs(1) - 1)
        def _():
            o_ref[...] = _rmsnorm(o_ref[...], gf_ref[...])


def _ffn(h, g, wg, wu, wd, g_final, casts=(), *, final_norm, tm=1024, tf=512, down_n=512):
    s, d = h.shape
    f = wg.shape[1]
    n_i, n_j = s // tm, f // tf
    band = lambda i, j: (jnp.minimum(i * n_j + j, -(-d // CAST_ROWS) - 1), 0)
    assert all(w.shape[0] == d and d % CAST_ROWS == 0 and d // CAST_ROWS <= n_i * n_j for w in casts)
    cast_specs = lambda: [pl.BlockSpec((CAST_ROWS, w.shape[1]), band) for w in casts]
    limit = _vmem_limit(
        _nbytes((tm, d), F32), 2 * _nbytes((tm, d), F32), _nbytes((tm, d), BF16),
        2 * 3 * _nbytes((d, tf), F32), 3 * _nbytes((d, tf), BF16),
        4 * _nbytes((tm, V7X_MXU_N), F32), _nbytes((tm, tf), BF16),
        2 * _nbytes((tm, down_n), F32), _nbytes((tm, d), F32),
        *[3 * _nbytes((CAST_ROWS, w.shape[1]), F32) for w in casts])
    outs = pl.pallas_call(
        functools.partial(_ffn_kernel, down_n=down_n, final_norm=final_norm, n_casts=len(casts)),
        grid=(n_i, n_j),
        in_specs=[
            pl.BlockSpec(memory_space=pl.ANY),
            pl.BlockSpec((1, d), lambda i, j: (0, 0)),
            pl.BlockSpec((d, tf), lambda i, j: (0, j)),
            pl.BlockSpec((d, tf), lambda i, j: (0, j)),
            pl.BlockSpec((tf, d), lambda i, j: (j, 0)),
            pl.BlockSpec((1, d), lambda i, j: (0, 0)),
            *cast_specs(),
        ],
        out_specs=[pl.BlockSpec((tm, d), lambda i, j: (i, 0)), *cast_specs()],
        out_shape=[jax.ShapeDtypeStruct((s, d), F32),
                   *[jax.ShapeDtypeStruct(w.shape, BF16) for w in casts]],
        scratch_shapes=[pltpu.VMEM((tm, d), BF16), pltpu.VMEM((tm, d), F32),
                        pltpu.SemaphoreType.DMA(())],
        compiler_params=pltpu.CompilerParams(dimension_semantics=_ARB2, vmem_limit_bytes=limit),
        name="ffn",
    )(h, g, wg, wu, wd, g_final, *casts)
    return outs


def _in_proj_kernel(h_ref, g_ref, w_hbm, u_ref, v_ref, q_ref, kv_ref, w_ref, w_sem):
    _load_resident([(w_hbm, w_ref, w_sem)])
    hn = _rmsnorm(h_ref[...], g_ref[...]).astype(BF16)
    d_a = u_ref.shape[1]
    d_b = q_ref.shape[1]

    def proj(lo):
        return _dot(hn, w_ref[:, lo:lo + V7X_MXU_N])

    for c in range(0, d_a, V7X_MXU_N):
        u_ref[:, c:c + V7X_MXU_N] = _gelu(proj(c)).astype(BF16)
    for c in range(0, d_a, V7X_MXU_N):
        v_ref[:, c:c + V7X_MXU_N] = _gelu(proj(d_a + c)).astype(BF16)
    for c in range(0, d_b, V7X_MXU_N):
        q_ref[:, c:c + V7X_MXU_N] = proj(2 * d_a + c).astype(BF16)
    for c in range(0, kv_ref.shape[1], V7X_MXU_N):
        kv_ref[:, c:c + V7X_MXU_N] = proj(2 * d_a + d_b + c).astype(BF16)


def _in_proj(h, g, w, d_a, d_b, d_kv, *, tm=1024):
    s, d = h.shape
    n = w.shape[1]
    limit = _vmem_limit(
        2 * _nbytes((tm, d), F32), _nbytes((d, n), BF16), 2 * _nbytes((tm, n), BF16),
        _nbytes((tm, d), F32), _nbytes((tm, d), BF16), 6 * _nbytes((tm, V7X_MXU_N), F32))
    return pl.pallas_call(
        _in_proj_kernel,
        grid=(s // tm,),
        in_specs=[
            pl.BlockSpec((tm, d), lambda i: (i, 0)),
            pl.BlockSpec((1, d), lambda i: (0, 0)),
            _HBM,
        ],
        out_specs=[
            pl.BlockSpec((tm, d_a), lambda i: (i, 0)),
            pl.BlockSpec((tm, d_a), lambda i: (i, 0)),
            pl.BlockSpec((tm, d_b), lambda i: (i, 0)),
            pl.BlockSpec((tm, d_kv), lambda i: (i, 0)),
        ],
        out_shape=[
            jax.ShapeDtypeStruct((s, d_a), BF16),
            jax.ShapeDtypeStruct((s, d_a), BF16),
            jax.ShapeDtypeStruct((s, d_b), BF16),
            jax.ShapeDtypeStruct((s, d_kv), BF16),
        ],
        scratch_shapes=[pltpu.VMEM((d, n), BF16), pltpu.SemaphoreType.DMA(())],
        compiler_params=pltpu.CompilerParams(dimension_semantics=_ARB1, vmem_limit_bytes=limit),
        name="in_proj",
    )(h, g, w)


def _mixer_kernel(sinks_ref, u_ref, v_ref, q_ref, kvc_ref, kvp_ref, h_ref, gv_ref, ws_ref, bst_ref,
                  ga_ref, gb_ref, wout_hbm, o_ref, kvbuf_ref, wsm_ref, y_ref, wout_ref, wout_sem):
    tm, d_a = u_ref.shape
    d_b = q_ref.shape[1]
    n_blocks = tm // CHUNK
    group = d_b // HEAD_DIM // B_KV_HEADS
    kv_w = B_KV_HEADS * HEAD_DIM
    step = pl.program_id(0)
    tile = jnp.minimum(step, pl.num_programs(0) - 2)
    y_new = y_ref.at[step % 2]
    y_old = y_ref.at[(step + 1) % 2]
    _load_resident([(wout_hbm, wout_ref, wout_sem)])

    @pl.when(step == 0)
    def _():
        y_ref[1] = jnp.zeros(y_ref.shape[1:], BF16)

    kvbuf_ref[0:CHUNK, :] = kvp_ref[...]
    kvbuf_ref[CHUNK:, :] = kvc_ref[...]

    t_idx = lax.broadcasted_iota(jnp.int32, (CHUNK, CHUNK), 0)
    s_idx = lax.broadcasted_iota(jnp.int32, (CHUNK, CHUNK), 1)
    for g in range(A_GROUPS):
        wsm_ref[g] = jnp.where(s_idx <= t_idx, ws_ref[g], 0.0).astype(BF16)

    low_half = lax.broadcasted_iota(jnp.int32, (CHUNK, V7X_LANES), 1) < HEAD_DIM
    low2 = lax.broadcasted_iota(jnp.int32, (2 * CHUNK, V7X_LANES), 1) < HEAD_DIM
    top_half = lax.broadcasted_iota(jnp.int32, (V7X_LANES, CHUNK), 0) < HEAD_DIM
    k_pos = lax.broadcasted_iota(jnp.int32, (2 * CHUNK, CHUNK), 0)
    q_pos = lax.broadcasted_iota(jnp.int32, (2 * CHUNK, CHUNK), 1)
    band = (k_pos > q_pos) & (k_pos <= q_pos + CHUNK)

    pieces = [(r, c) for c in range(0, o_ref.shape[1], PROJ_N) for r in range(0, tm, PROJ_M)]
    n_pieces, n_stages = len(pieces), n_blocks * (1 + B_KV_HEADS * (group // 2))
    stage = [0]

    def project():
        stage[0] += 1
        while n_pieces - len(pieces) < stage[0] * n_pieces // n_stages:
            r, c = pieces.pop(0)
            rows, cols = slice(r, r + PROJ_M), slice(c, c + PROJ_N)
            o_ref[rows, cols] = h_ref[rows, cols] + _dot(y_old[rows, :], wout_ref[:, cols])

    for b in range(n_blocks):
        r0 = b * CHUNK
        blk = slice(r0, r0 + CHUNK)

        project()
        vn = _rmsnorm(v_ref[blk, :].astype(F32), gv_ref[...]).astype(BF16)
        ub = u_ref[blk, :].astype(F32)
        ya = []
        for g in range(A_GROUPS):
            cols = slice(g * CHUNK, (g + 1) * CHUNK)
            sv = _dot(wsm_ref[g], vn[:, cols]) + bst_ref[:, g:g + 1]
            ya.append(ub[:, cols] * sv)
        ya = jnp.concatenate(ya, axis=1)
        y_new[blk, 0:d_a] = _rmsnorm(ya, ga_ref[...]).astype(BF16)

        qb = q_ref[blk, :] * (HEAD_DIM ** -0.5)
        kk = kvbuf_ref[r0:r0 + 2 * CHUNK, :]
        k_f = kk[:, 0:kv_w].astype(F32)
        v_f = kk[:, kv_w:2 * kv_w].astype(F32)
        k_rot = pltpu.roll(k_f, HEAD_DIM, axis=1)
        v_rot = pltpu.roll(v_f, HEAD_DIM, axis=1)
        mask = band & (k_pos >= jnp.where(tile == 0, CHUNK, 0)) if b == 0 else band
        ybt = []
        for j in range(B_KV_HEADS):
            if j == 0:
                k2 = jnp.where(low2, k_f, k_rot).astype(BF16)
                v2t = jnp.where(low2, v_f, v_rot).T.astype(BF16)
            else:
                k2 = jnp.where(low2, k_rot, k_f).astype(BF16)
                v2t = jnp.where(low2, v_rot, v_f).T.astype(BF16)
            qm = []
            for hh in range(group):
                head = j * group + hh
                c0 = (head // 2) * V7X_LANES
                qc = qb[:, c0:c0 + V7X_LANES]
                keep = low_half if head % 2 == 0 else jnp.logical_not(low_half)
                qm.append(jnp.where(keep, qc, jnp.zeros_like(qc)))
            st = _dot_nt(k2, jnp.concatenate(qm, axis=0))
            pt = []
            for hh in range(group):
                sink = sinks_ref[j * group + hh]
                sc = jnp.where(mask, st[:, hh * CHUNK:(hh + 1) * CHUNK], NEG)
                m = jnp.maximum(jnp.max(sc, axis=0, keepdims=True), sink)
                e = jnp.exp(sc - m)
                denom = jnp.sum(e, axis=0, keepdims=True) + jnp.exp(sink - m)
                pt.append((e * (1.0 / denom)).astype(BF16))
                if hh % 2 == 1:
                    project()
            ot = _dot(v2t, jnp.concatenate(pt, axis=1))
            for pair in range(group // 2):
                even = ot[:, (2 * pair) * CHUNK:(2 * pair + 1) * CHUNK]
                odd = ot[:, (2 * pair + 1) * CHUNK:(2 * pair + 2) * CHUNK]
                ybt.append(jnp.where(top_half, even, odd))
        ybt = jnp.concatenate(ybt, axis=0)
        ybt = ybt * lax.rsqrt(jnp.mean(ybt * ybt, axis=0, keepdims=True) + EPS)
        y_new[blk, d_a:d_a + d_b] = (ybt.T * gb_ref[...]).astype(BF16)
    assert not pieces


def _mixer(sinks, u, v, q, kv, h, g_v, w_s, b_s_t, g_a, g_b, w_out, *, tm=512):
    s, d = h.shape
    d_a = u.shape[1]
    d_b = q.shape[1]
    d_kv = kv.shape[1]
    d_mix = d_a + d_b
    n_blocks = tm // CHUNK
    n_tiles = s // tm
    limit = _vmem_limit(
        2 * 2 * _nbytes((tm, d_a), BF16), 2 * _nbytes((tm, d_b), BF16),
        2 * _nbytes((tm + CHUNK, d_kv), BF16), 4 * _nbytes((tm, d), F32),
        _nbytes((d_mix, d), BF16), 2 * _nbytes((tm, d_mix), BF16),
        _nbytes((tm + CHUNK, d_kv), BF16), 2 * _nbytes((A_GROUPS, CHUNK, CHUNK), F32),
        2 * _nbytes((tm, d // n_blocks), F32), 32 * _nbytes((CHUNK, d_a), F32))
    vec = lambda n: pl.BlockSpec((1, n), lambda i: (0, 0))
    mixed = lambda i: (jnp.minimum(i, n_tiles - 1), 0)
    projected = lambda i: (jnp.maximum(i - 1, 0), 0)
    prev_block = lambda i: (jnp.maximum(jnp.minimum(i, n_tiles - 1) * n_blocks - 1, 0), 0)
    return pl.pallas_call(
        _mixer_kernel,
        grid=(n_tiles + 1,),
        in_specs=[
            pl.BlockSpec(memory_space=pltpu.SMEM),
            pl.BlockSpec((tm, d_a), mixed),
            pl.BlockSpec((tm, d_a), mixed),
            pl.BlockSpec((tm, d_b), mixed),
            pl.BlockSpec((tm, d_kv), mixed),
            pl.BlockSpec((CHUNK, d_kv), prev_block),
            pl.BlockSpec((tm, d), projected),
            vec(d_a),
            pl.BlockSpec((A_GROUPS, CHUNK, CHUNK), lambda i: (0, 0, 0)),
            pl.BlockSpec((CHUNK, A_GROUPS), lambda i: (0, 0)),
            vec(d_a),
            vec(d_b),
            _HBM,
        ],
        out_specs=pl.BlockSpec((tm, d), projected),
        out_shape=jax.ShapeDtypeStruct((s, d), F32),
        scratch_shapes=[
            pltpu.VMEM((tm + CHUNK, d_kv), BF16),
            pltpu.VMEM((A_GROUPS, CHUNK, CHUNK), BF16),
            pltpu.VMEM((2, tm, d_mix), BF16),
            pltpu.VMEM((d_mix, d), BF16),
            pltpu.SemaphoreType.DMA(()),
        ],
        compiler_params=pltpu.CompilerParams(dimension_semantics=_ARB1, vmem_limit_bytes=limit),
        name="mixer",
    )(sinks, u, v, q, kv, kv, h, g_v, w_s, b_s_t, g_a, g_b, w_out)


def _mem_kv_kernel(mem_ref, g_ref, w_ref, kv_ref):
    mn = _rmsnorm(mem_ref[...], g_ref[...]).astype(BF16)
    kv_ref[...] = _dot(mn, w_ref[...].astype(BF16)).astype(BF16)


def _mem_kv(mem, g, w, *, tn=512):
    m, d = mem.shape
    n = w.shape[1]
    limit = _vmem_limit(
        2 * _nbytes((m, d), F32), 2 * _nbytes((d, tn), F32), _nbytes((d, tn), BF16),
        2 * _nbytes((m, tn), BF16), 3 * _nbytes((m, d), F32), 2 * _nbytes((m, tn), F32))
    return pl.pallas_call(
        _mem_kv_kernel,
        grid=(n // tn,),
        in_specs=[
            pl.BlockSpec((m, d), lambda j: (0, 0)),
            pl.BlockSpec((1, d), lambda j: (0, 0)),
            pl.BlockSpec((d, tn), lambda j: (0, j)),
        ],
        out_specs=pl.BlockSpec((m, tn), lambda j: (0, j)),
        out_shape=jax.ShapeDtypeStruct((m, n), BF16),
        compiler_params=pltpu.CompilerParams(dimension_semantics=_ARB1, vmem_limit_bytes=limit),
        name="mem_kv",
    )(mem, g, w)


def _cross_kernel(h_ref, g_ref, wq_hbm, kv_hbm, wo_hbm, o_ref, wq_ref, kv_ref, wo_ref, sems):
    _load_resident([(wq_hbm, wq_ref, sems.at[0]), (kv_hbm, kv_ref, sems.at[1]),
                    (wo_hbm, wo_ref, sems.at[2])])
    x = h_ref[...]
    d = x.shape[1]
    hd = d // X_HEADS
    hn = _rmsnorm(x, g_ref[...]).astype(BF16)
    outs = []
    for head in range(X_HEADS):
        cols = slice(head * hd, (head + 1) * hd)
        q = _dot(hn, wq_ref[:, cols]).astype(BF16)
        sc = _dot_nt(q, kv_ref[:, cols]) * (hd ** -0.5)
        m = jnp.max(sc, axis=1, keepdims=True)
        e = jnp.exp(sc - m)
        p = (e * (1.0 / jnp.sum(e, axis=1, keepdims=True))).astype(BF16)
        outs.append(_dot(p, kv_ref[:, d + head * hd:d + (head + 1) * hd]).astype(BF16))
    o_ref[...] = x + _dot(jnp.concatenate(outs, axis=1), wo_ref[...])


def _cross(h, g, wq, kv, wo, *, tm=512):
    s, d = h.shape
    m = kv.shape[0]
    limit = _vmem_limit(
        4 * _nbytes((tm, d), F32), 2 * _nbytes((d, d), BF16), _nbytes((m, 2 * d), BF16),
        3 * _nbytes((tm, d), F32), 2 * _nbytes((tm, d), BF16), 4 * _nbytes((tm, d // X_HEADS), F32))
    return pl.pallas_call(
        _cross_kernel,
        grid=(s // tm,),
        in_specs=[
            pl.BlockSpec((tm, d), lambda i: (i, 0)),
            pl.BlockSpec((1, d), lambda i: (0, 0)),
            _HBM, _HBM, _HBM,
        ],
        out_specs=pl.BlockSpec((tm, d), lambda i: (i, 0)),
        out_shape=jax.ShapeDtypeStruct((s, d), F32),
        scratch_shapes=[pltpu.VMEM((d, d), BF16), pltpu.VMEM((m, 2 * d), BF16),
                        pltpu.VMEM((d, d), BF16), pltpu.SemaphoreType.DMA((3,))],
        compiler_params=pltpu.CompilerParams(dimension_semantics=_ARB1, vmem_limit_bytes=limit),
        name="cross_attn",
    )(h, g, wq, kv, wo)


def _row(v):
    return v.reshape(1, -1)


def kernel(x, mem, g_ffn1, w1_gate, w1_up, w1_down, g_mix, w_in, g_v, w_s, b_s, sinks, g_a_out,
           g_b_out, w_out, g_x, g_mem, w_xq, w_xkv, w_xo, g_ffn2, w2_gate, w2_up, w2_down, g_final):
    depth = g_ffn1.shape[0]
    d_a = g_v.shape[1]
    d_b = g_b_out.shape[1]
    d_kv = 2 * B_KV_HEADS * HEAD_DIM
    outs = []
    bf16_weights = {}
    for b in range(x.shape[0]):
        h = x[b]
        for l in range(depth):
            casts = () if l in bf16_weights else (w_in[l], w_out[l], w_xq[l], w_xo[l])
            h, *copies = _ffn(h, _row(g_ffn1[l]), w1_gate[l], w1_up[l], w1_down[l], _row(g_final),
                              casts, final_norm=False)
            w_in16, w_out16, w_xq16, w_xo16 = bf16_weights.setdefault(l, copies)

            u, v, q, kv = _in_proj(h, _row(g_mix[l]), w_in16, d_a, d_b, d_kv)
            h = _mixer(sinks[l], u, v, q, kv, h, _row(g_v[l]), w_s[l], jnp.transpose(b_s[l]),
                       _row(g_a_out[l]), _row(g_b_out[l]), w_out16)

            mem_kv = _mem_kv(mem[b], _row(g_mem[l]), w_xkv[l])
            h = _cross(h, _row(g_x[l]), w_xq16, mem_kv, w_xo16)

            h, = _ffn(h, _row(g_ffn2[l]), w2_gate[l], w2_up[l], w2_down[l], _row(g_final),
                      final_norm=(l == depth - 1))
        outs.append(h)
    return outs[0][None] if len(outs) == 1 else jnp.stack(outs, axis=0)
```

```python
import functools

import jax
import jax.numpy as jnp
from jax import lax
from jax.experimental import pallas as pl
from jax.experimental.pallas import tpu as pltpu

F32 = jnp.float32
BF16 = jnp.bfloat16

EPS = 1e-5
NEG = -1e30
CHUNK = 128
A_GROUPS = 8
HEAD_DIM = 64
B_KV_HEADS = 2
X_HEADS = 4

V7X_VMEM_BYTES = 64 * 2**20
V7X_LANES = 128
V7X_MXU_N = 256
CAST_ROWS = 32
PROJ_M = 256
PROJ_N = 256


def _vmem_limit(*byte_counts):
    return int(min(sum(byte_counts), V7X_VMEM_BYTES))


def _nbytes(shape, dtype):
    n = 1
    for s in shape:
        n *= s
    return n * jnp.dtype(dtype).itemsize


def _rmsnorm(x, g):
    ms = jnp.mean(x * x, axis=-1, keepdims=True)
    return x * lax.rsqrt(ms + EPS) * g


def _gelu(x):
    return 0.5 * x * (1.0 + lax.erf(x * (2.0 ** -0.5)))


def _dot(a, b):
    return jnp.dot(a, b, preferred_element_type=F32)


def _dot_nt(a, b):
    return lax.dot_general(a, b, (((1,), (1,)), ((), ())), preferred_element_type=F32)


def _load_resident(pairs):
    copies = [pltpu.make_async_copy(src, dst, sem) for src, dst, sem in pairs]

    @pl.when(pl.program_id(0) == 0)
    def _():
        for cp in copies:
            cp.start()
        for cp in copies:
            cp.wait()


_HBM = pl.BlockSpec(memory_space=pl.ANY)
_ARB1 = ("arbitrary",)
_ARB2 = ("arbitrary", "arbitrary")


def _ffn_kernel(h_hbm, g_ref, wg_ref, wu_ref, wd_ref, gf_ref, *rest, down_n, final_norm, n_casts):
    cast_in, o_ref, cast_out = rest[:n_casts], rest[n_casts], rest[n_casts + 1:2 * n_casts + 1]
    hn_ref, hbuf_ref, h_sem = rest[2 * n_casts + 1:]
    i, j = pl.program_id(0), pl.program_id(1)
    tm = hbuf_ref.shape[0]

    def h_copy(tile):
        rows = pl.ds(pl.multiple_of(tile * tm, tm), tm)
        return pltpu.make_async_copy(h_hbm.at[rows, :], hbuf_ref, h_sem)

    for src, dst in zip(cast_in, cast_out):
        dst[...] = src[...].astype(BF16)

    @pl.when(jnp.logical_and(i == 0, j == 0))
    def _():
        h_copy(0).start()

    @pl.when(j == 0)
    def _():
        h_copy(i).wait()
        x = hbuf_ref[...]
        hn_ref[...] = _rmsnorm(x, g_ref[...]).astype(BF16)
        o_ref[...] = x

    @pl.when(jnp.logical_and(j == 1, i + 1 < pl.num_programs(0)))
    def _():
        h_copy(i + 1).start()

    hn = hn_ref[...]
    tf = wg_ref.shape[1]
    act = []
    for c in range(0, tf, V7X_MXU_N):
        cols = slice(c, c + V7X_MXU_N)
        gate = _dot(hn, wg_ref[:, cols].astype(BF16))
        up = _dot(hn, wu_ref[:, cols].astype(BF16))
        act.append(((0.5 * jax.nn.silu(gate)) * up).astype(BF16))
    act = act[0] if len(act) == 1 else jnp.concatenate(act, axis=1)
    for n in range(0, o_ref.shape[1], down_n):
        cols = slice(n, n + down_n)
        o_ref[:, cols] += _dot(act, wd_ref[:, cols].astype(BF16))

    if final_norm:
        @pl.when(j == pl.num_programs(1) - 1)
        def _():
            o_ref[...] = _rmsnorm(o_ref[...], gf_ref[...])


def _ffn(h, g, wg, wu, wd, g_final, casts=(), *, final_norm, tm=1024, tf=512, down_n=512):
    s, d = h.shape
    f = wg.shape[1]
    n_i, n_j = s // tm, f // tf
    band = lambda i, j: (jnp.minimum(i * n_j + j, -(-d // CAST_ROWS) - 1), 0)
    assert all(w.shape[0] == d and d % CAST_ROWS == 0 and d // CAST_ROWS <= n_i * n_j for w in casts)
    cast_specs = lambda: [pl.BlockSpec((CAST_ROWS, w.shape[1]), band) for w in casts]
    limit = _vmem_limit(
        _nbytes((tm, d), F32), 2 * _nbytes((tm, d), F32), _nbytes((tm, d), BF16),
        2 * 3 * _nbytes((d, tf), F32), 3 * _nbytes((d, tf), BF16),
        4 * _nbytes((tm, V7X_MXU_N), F32), _nbytes((tm, tf), BF16),
        2 * _nbytes((tm, down_n), F32), _nbytes((tm, d), F32),
        *[3 * _nbytes((CAST_ROWS, w.shape[1]), F32) for w in casts])
    outs = pl.pallas_call(
        functools.partial(_ffn_kernel, down_n=down_n, final_norm=final_norm, n_casts=len(casts)),
        grid=(n_i, n_j),
        in_specs=[
            pl.BlockSpec(memory_space=pl.ANY),
            pl.BlockSpec((1, d), lambda i, j: (0, 0)),
            pl.BlockSpec((d, tf), lambda i, j: (0, j)),
            pl.BlockSpec((d, tf), lambda i, j: (0, j)),
            pl.BlockSpec((tf, d), lambda i, j: (j, 0)),
            pl.BlockSpec((1, d), lambda i, j: (0, 0)),
            *cast_specs(),
        ],
        out_specs=[pl.BlockSpec((tm, d), lambda i, j: (i, 0)), *cast_specs()],
        out_shape=[jax.ShapeDtypeStruct((s, d), F32),
                   *[jax.ShapeDtypeStruct(w.shape, BF16) for w in casts]],
        scratch_shapes=[pltpu.VMEM((tm, d), BF16), pltpu.VMEM((tm, d), F32),
                        pltpu.SemaphoreType.DMA(())],
        compiler_params=pltpu.CompilerParams(dimension_semantics=_ARB2, vmem_limit_bytes=limit),
        name="ffn",
    )(h, g, wg, wu, wd, g_final, *casts)
    return outs


def _in_proj_kernel(h_ref, g_ref, w_hbm, u_ref, v_ref, q_ref, kv_ref, w_ref, w_sem):
    _load_resident([(w_hbm, w_ref, w_sem)])
    hn = _rmsnorm(h_ref[...], g_ref[...]).astype(BF16)
    d_a = u_ref.shape[1]
    d_b = q_ref.shape[1]

    def proj(lo):
        return _dot(hn, w_ref[:, lo:lo + V7X_MXU_N])

    for c in range(0, d_a, V7X_MXU_N):
        u_ref[:, c:c + V7X_MXU_N] = _gelu(proj(c)).astype(BF16)
    for c in range(0, d_a, V7X_MXU_N):
        v_ref[:, c:c + V7X_MXU_N] = _gelu(proj(d_a + c)).astype(BF16)
    for c in range(0, d_b, V7X_MXU_N):
        q_ref[:, c:c + V7X_MXU_N] = proj(2 * d_a + c).astype(BF16)
    for c in range(0, kv_ref.shape[1], V7X_MXU_N):
        kv_ref[:, c:c + V7X_MXU_N] = proj(2 * d_a + d_b + c).astype(BF16)


def _in_proj(h, g, w, d_a, d_b, d_kv, *, tm=1024):
    s, d = h.shape
    n = w.shape[1]
    limit = _vmem_limit(
        2 * _nbytes((tm, d), F32), _nbytes((d, n), BF16), 2 * _nbytes((tm, n), BF16),
        _nbytes((tm, d), F32), _nbytes((tm, d), BF16), 6 * _nbytes((tm, V7X_MXU_N), F32))
    return pl.pallas_call(
        _in_proj_kernel,
        grid=(s // tm,),
        in_specs=[
            pl.BlockSpec((tm, d), lambda i: (i, 0)),
            pl.BlockSpec((1, d), lambda i: (0, 0)),
            _HBM,
        ],
        out_specs=[
            pl.BlockSpec((tm, d_a), lambda i: (i, 0)),
            pl.BlockSpec((tm, d_a), lambda i: (i, 0)),
            pl.BlockSpec((tm, d_b), lambda i: (i, 0)),
            pl.BlockSpec((tm, d_kv), lambda i: (i, 0)),
        ],
        out_shape=[
            jax.ShapeDtypeStruct((s, d_a), BF16),
            jax.ShapeDtypeStruct((s, d_a), BF16),
            jax.ShapeDtypeStruct((s, d_b), BF16),
            jax.ShapeDtypeStruct((s, d_kv), BF16),
        ],
        scratch_shapes=[pltpu.VMEM((d, n), BF16), pltpu.SemaphoreType.DMA(())],
        compiler_params=pltpu.CompilerParams(dimension_semantics=_ARB1, vmem_limit_bytes=limit),
        name="in_proj",
    )(h, g, w)


def _mixer_kernel(sinks_ref, u_ref, v_ref, q_ref, kvc_ref, kvp_ref, h_ref, gv_ref, ws_ref, bst_ref,
                  ga_ref, gb_ref, wout_hbm, o_ref, kvbuf_ref, wsm_ref, y_new, y_old, wout_ref,
                  wout_sem):
    tm, d_a = u_ref.shape
    d_b = q_ref.shape[1]
    n_blocks = tm // CHUNK
    group = d_b // HEAD_DIM // B_KV_HEADS
    kv_w = B_KV_HEADS * HEAD_DIM
    step = pl.program_id(0)
    tile = jnp.minimum(step, pl.num_programs(0) - 2)
    _load_resident([(wout_hbm, wout_ref, wout_sem)])

    @pl.when(step == 0)
    def _():
        y_new[...] = jnp.zeros(y_new.shape, BF16)
        t_idx = lax.broadcasted_iota(jnp.int32, (CHUNK, CHUNK), 0)
        s_idx = lax.broadcasted_iota(jnp.int32, (CHUNK, CHUNK), 1)
        for g in range(A_GROUPS):
            wsm_ref[g] = jnp.where(s_idx <= t_idx, ws_ref[g], 0.0).astype(BF16)

    y_old[...] = y_new[...]

    kvbuf_ref[0:CHUNK, :] = kvp_ref[...]
    kvbuf_ref[CHUNK:, :] = kvc_ref[...]

    low_half = lax.broadcasted_iota(jnp.int32, (CHUNK, V7X_LANES), 1) < HEAD_DIM
    low2 = lax.broadcasted_iota(jnp.int32, (2 * CHUNK, V7X_LANES), 1) < HEAD_DIM
    top_half = lax.broadcasted_iota(jnp.int32, (V7X_LANES, CHUNK), 0) < HEAD_DIM
    k_pos = lax.broadcasted_iota(jnp.int32, (2 * CHUNK, CHUNK), 0)
    q_pos = lax.broadcasted_iota(jnp.int32, (2 * CHUNK, CHUNK), 1)
    band = (k_pos > q_pos) & (k_pos <= q_pos + CHUNK)

    pieces = [(r, c) for c in range(0, o_ref.shape[1], PROJ_N) for r in range(0, tm, PROJ_M)]
    n_pieces, n_stages = len(pieces), n_blocks * (1 + B_KV_HEADS * (group // 2))
    stage = [0]

    def project():
        stage[0] += 1
        while n_pieces - len(pieces) < stage[0] * n_pieces // n_stages:
            r, c = pieces.pop(0)
            rows, cols = slice(r, r + PROJ_M), slice(c, c + PROJ_N)
            o_ref[rows, cols] = h_ref[rows, cols] + _dot(y_old[rows, :], wout_ref[:, cols])

    for b in range(n_blocks):
        r0 = b * CHUNK
        blk = slice(r0, r0 + CHUNK)

        project()
        vn = _rmsnorm(v_ref[blk, :].astype(F32), gv_ref[...]).astype(BF16)
        ub = u_ref[blk, :].astype(F32)
        ya = []
        for g in range(A_GROUPS):
            cols = slice(g * CHUNK, (g + 1) * CHUNK)
            sv = _dot(wsm_ref[g], vn[:, cols]) + bst_ref[:, g:g + 1]
            ya.append(ub[:, cols] * sv)
        ya = jnp.concatenate(ya, axis=1)
        y_new[blk, 0:d_a] = _rmsnorm(ya, ga_ref[...]).astype(BF16)

        qb = q_ref[blk, :] * (HEAD_DIM ** -0.5)
        kk = kvbuf_ref[r0:r0 + 2 * CHUNK, :]
        k_f = kk[:, 0:kv_w].astype(F32)
        v_f = kk[:, kv_w:2 * kv_w].astype(F32)
        k_rot = pltpu.roll(k_f, HEAD_DIM, axis=1)
        v_rot = pltpu.roll(v_f, HEAD_DIM, axis=1)
        mask = band & (k_pos >= jnp.where(tile == 0, CHUNK, 0)) if b == 0 else band
        ybt = []
        for j in range(B_KV_HEADS):
            if j == 0:
                k2 = jnp.where(low2, k_f, k_rot).astype(BF16)
                v2t = jnp.where(low2, v_f, v_rot).T.astype(BF16)
            else:
                k2 = jnp.where(low2, k_rot, k_f).astype(BF16)
                v2t = jnp.where(low2, v_rot, v_f).T.astype(BF16)
            qm = []
            for hh in range(group):
                head = j * group + hh
                c0 = (head // 2) * V7X_LANES
                qc = qb[:, c0:c0 + V7X_LANES]
                keep = low_half if head % 2 == 0 else jnp.logical_not(low_half)
                qm.append(jnp.where(keep, qc, jnp.zeros_like(qc)))
            st = _dot_nt(k2, jnp.concatenate(qm, axis=0))
            pt = []
            for hh in range(group):
                sink = sinks_ref[j * group + hh]
                sc = jnp.where(mask, st[:, hh * CHUNK:(hh + 1) * CHUNK], NEG)
                m = jnp.maximum(jnp.max(sc, axis=0, keepdims=True), sink)
                e = jnp.exp(sc - m)
                denom = jnp.sum(e, axis=0, keepdims=True) + jnp.exp(sink - m)
                pt.append((e * (1.0 / denom)).astype(BF16))
                if hh % 2 == 1:
                    project()
            ot = _dot(v2t, jnp.concatenate(pt, axis=1))
            for pair in range(group // 2):
                even = ot[:, (2 * pair) * CHUNK:(2 * pair + 1) * CHUNK]
                odd = ot[:, (2 * pair + 1) * CHUNK:(2 * pair + 2) * CHUNK]
                ybt.append(jnp.where(top_half, even, odd))
        ybt = jnp.concatenate(ybt, axis=0)
        ybt = ybt * lax.rsqrt(jnp.mean(ybt * ybt, axis=0, keepdims=True) + EPS)
        y_new[blk, d_a:d_a + d_b] = (ybt.T * gb_ref[...]).astype(BF16)
    assert not pieces


def _mixer(sinks, u, v, q, kv, h, g_v, w_s, b_s_t, g_a, g_b, w_out, *, tm=512):
    s, d = h.shape
    d_a = u.shape[1]
    d_b = q.shape[1]
    d_kv = kv.shape[1]
    d_mix = d_a + d_b
    n_blocks = tm // CHUNK
    n_tiles = s // tm
    limit = _vmem_limit(
        2 * 2 * _nbytes((tm, d_a), BF16), 2 * _nbytes((tm, d_b), BF16),
        2 * _nbytes((tm + CHUNK, d_kv), BF16), 4 * _nbytes((tm, d), F32),
        _nbytes((d_mix, d), BF16), 2 * _nbytes((tm, d_mix), BF16),
        _nbytes((tm + CHUNK, d_kv), BF16), 2 * _nbytes((A_GROUPS, CHUNK, CHUNK), F32),
        2 * _nbytes((tm, d // n_blocks), F32), 32 * _nbytes((CHUNK, d_a), F32))
    vec = lambda n: pl.BlockSpec((1, n), lambda i: (0, 0))
    mixed = lambda i: (jnp.minimum(i, n_tiles - 1), 0)
    projected = lambda i: (jnp.maximum(i - 1, 0), 0)
    prev_block = lambda i: (jnp.maximum(jnp.minimum(i, n_tiles - 1) * n_blocks - 1, 0), 0)
    return pl.pallas_call(
        _mixer_kernel,
        grid=(n_tiles + 1,),
        in_specs=[
            pl.BlockSpec(memory_space=pltpu.SMEM),
            pl.BlockSpec((tm, d_a), mixed),
            pl.BlockSpec((tm, d_a), mixed),
            pl.BlockSpec((tm, d_b), mixed),
            pl.BlockSpec((tm, d_kv), mixed),
            pl.BlockSpec((CHUNK, d_kv), prev_block),
            pl.BlockSpec((tm, d), projected),
            vec(d_a),
            pl.BlockSpec((A_GROUPS, CHUNK, CHUNK), lambda i: (0, 0, 0)),
            pl.BlockSpec((CHUNK, A_GROUPS), lambda i: (0, 0)),
            vec(d_a),
            vec(d_b),
            _HBM,
        ],
        out_specs=pl.BlockSpec((tm, d), projected),
        out_shape=jax.ShapeDtypeStruct((s, d), F32),
        scratch_shapes=[
            pltpu.VMEM((tm + CHUNK, d_kv), BF16),
            pltpu.VMEM((A_GROUPS, CHUNK, CHUNK), BF16),
            pltpu.VMEM((tm, d_mix), BF16),
            pltpu.VMEM((tm, d_mix), BF16),
            pltpu.VMEM((d_mix, d), BF16),
            pltpu.SemaphoreType.DMA(()),
        ],
        compiler_params=pltpu.CompilerParams(dimension_semantics=_ARB1, vmem_limit_bytes=limit),
        name="mixer",
    )(sinks, u, v, q, kv, kv, h, g_v, w_s, b_s_t, g_a, g_b, w_out)


def _mem_kv_kernel(mem_ref, g_ref, w_ref, kv_ref):
    mn = _rmsnorm(mem_ref[...], g_ref[...]).astype(BF16)
    kv_ref[...] = _dot(mn, w_ref[...].astype(BF16)).astype(BF16)


def _mem_kv(mem, g, w, *, tn=512):
    m, d = mem.shape
    n = w.shape[1]
    limit = _vmem_limit(
        2 * _nbytes((m, d), F32), 2 * _nbytes((d, tn), F32), _nbytes((d, tn), BF16),
        2 * _nbytes((m, tn), BF16), 3 * _nbytes((m, d), F32), 2 * _nbytes((m, tn), F32))
    return pl.pallas_call(
        _mem_kv_kernel,
        grid=(n // tn,),
        in_specs=[
            pl.BlockSpec((m, d), lambda j: (0, 0)),
            pl.BlockSpec((1, d), lambda j: (0, 0)),
            pl.BlockSpec((d, tn), lambda j: (0, j)),
        ],
        out_specs=pl.BlockSpec((m, tn), lambda j: (0, j)),
        out_shape=jax.ShapeDtypeStruct((m, n), BF16),
        compiler_params=pltpu.CompilerParams(dimension_semantics=_ARB1, vmem_limit_bytes=limit),
        name="mem_kv",
    )(mem, g, w)


def _cross_kernel(h_ref, g_ref, wq_hbm, kv_hbm, wo_hbm, o_ref, wq_ref, kv_ref, wo_ref, sems):
    _load_resident([(wq_hbm, wq_ref, sems.at[0]), (kv_hbm, kv_ref, sems.at[1]),
                    (wo_hbm, wo_ref, sems.at[2])])
    x = h_ref[...]
    d = x.shape[1]
    hd = d // X_HEADS
    hn = _rmsnorm(x, g_ref[...]).astype(BF16)
    outs = []
    for head in range(X_HEADS):
        cols = slice(head * hd, (head + 1) * hd)
        q = _dot(hn, wq_ref[:, cols]).astype(BF16)
        sc = _dot_nt(q, kv_ref[:, cols]) * (hd ** -0.5)
        m = jnp.max(sc, axis=1, keepdims=True)
        e = jnp.exp(sc - m)
        p = (e * (1.0 / jnp.sum(e, axis=1, keepdims=True))).astype(BF16)
        outs.append(_dot(p, kv_ref[:, d + head * hd:d + (head + 1) * hd]).astype(BF16))
    o_ref[...] = x + _dot(jnp.concatenate(outs, axis=1), wo_ref[...])


def _cross(h, g, wq, kv, wo, *, tm=512):
    s, d = h.shape
    m = kv.shape[0]
    limit = _vmem_limit(
        4 * _nbytes((tm, d), F32), 2 * _nbytes((d, d), BF16), _nbytes((m, 2 * d), BF16),
        3 * _nbytes((tm, d), F32), 2 * _nbytes((tm, d), BF16), 4 * _nbytes((tm, d // X_HEADS), F32))
    return pl.pallas_call(
        _cross_kernel,
        grid=(s // tm,),
        in_specs=[
            pl.BlockSpec((tm, d), lambda i: (i, 0)),
            pl.BlockSpec((1, d), lambda i: (0, 0)),
            _HBM, _HBM, _HBM,
        ],
        out_specs=pl.BlockSpec((tm, d), lambda i: (i, 0)),
        out_shape=jax.ShapeDtypeStruct((s, d), F32),
        scratch_shapes=[pltpu.VMEM((d, d), BF16), pltpu.VMEM((m, 2 * d), BF16),
                        pltpu.VMEM((d, d), BF16), pltpu.SemaphoreType.DMA((3,))],
        compiler_params=pltpu.CompilerParams(dimension_semantics=_ARB1, vmem_limit_bytes=limit),
        name="cross_attn",
    )(h, g, wq, kv, wo)


def _row(v):
    return v.reshape(1, -1)


def kernel(x, mem, g_ffn1, w1_gate, w1_up, w1_down, g_mix, w_in, g_v, w_s, b_s, sinks, g_a_out,
           g_b_out, w_out, g_x, g_mem, w_xq, w_xkv, w_xo, g_ffn2, w2_gate, w2_up, w2_down, g_final):
    depth = g_ffn1.shape[0]
    d_a = g_v.shape[1]
    d_b = g_b_out.shape[1]
    d_kv = 2 * B_KV_HEADS * HEAD_DIM
    outs = []
    bf16_weights = {}
    for b in range(x.shape[0]):
        h = x[b]
        for l in range(depth):
            casts = () if l in bf16_weights else (w_in[l], w_out[l], w_xq[l], w_xo[l])
            h, *copies = _ffn(h, _row(g_ffn1[l]), w1_gate[l], w1_up[l], w1_down[l], _row(g_final),
                              casts, final_norm=False)
            w_in16, w_out16, w_xq16, w_xo16 = bf16_weights.setdefault(l, copies)

            u, v, q, kv = _in_proj(h, _row(g_mix[l]), w_in16, d_a, d_b, d_kv)
            h = _mixer(sinks[l], u, v, q, kv, h, _row(g_v[l]), w_s[l], jnp.transpose(b_s[l]),
                       _row(g_a_out[l]), _row(g_b_out[l]), w_out16)

            mem_kv = _mem_kv(mem[b], _row(g_mem[l]), w_xkv[l])
            h = _cross(h, _row(g_x[l]), w_xq16, mem_kv, w_xo16)

            h, = _ffn(h, _row(g_ffn2[l]), w2_gate[l], w2_up[l], w2_down[l], _row(g_final),
                      final_norm=(l == depth - 1))
        outs.append(h)
    return outs[0][None] if len(outs) == 1 else jnp.stack(outs, axis=0)
```

```python
import functools

import jax
import jax.numpy as jnp
from jax import lax
from jax.experimental import pallas as pl
from jax.experimental.pallas import tpu as pltpu

F32 = jnp.float32
BF16 = jnp.bfloat16

EPS = 1e-5
NEG = -1e30
CHUNK = 128
A_GROUPS = 8
HEAD_DIM = 64
B_KV_HEADS = 2
X_HEADS = 4

V7X_VMEM_BYTES = 64 * 2**20
V7X_LANES = 128
V7X_MXU_N = 256
CAST_ROWS = 32
PROJ_M = 256
PROJ_N = 256


def _vmem_limit(*byte_counts):
    return int(min(sum(byte_counts), V7X_VMEM_BYTES))


def _nbytes(shape, dtype):
    n = 1
    for s in shape:
        n *= s
    return n * jnp.dtype(dtype).itemsize


def _rmsnorm(x, g):
    ms = jnp.mean(x * x, axis=-1, keepdims=True)
    return x * lax.rsqrt(ms + EPS) * g


def _gelu(x):
    return 0.5 * x * (1.0 + lax.erf(x * (2.0 ** -0.5)))


def _dot(a, b):
    return jnp.dot(a, b, preferred_element_type=F32)


def _dot_nt(a, b):
    return lax.dot_general(a, b, (((1,), (1,)), ((), ())), preferred_element_type=F32)


def _load_resident(pairs):
    copies = [pltpu.make_async_copy(src, dst, sem) for src, dst, sem in pairs]

    @pl.when(pl.program_id(0) == 0)
    def _():
        for cp in copies:
            cp.start()
        for cp in copies:
            cp.wait()


_HBM = pl.BlockSpec(memory_space=pl.ANY)
_ARB1 = ("arbitrary",)
_ARB2 = ("arbitrary", "arbitrary")


def _ffn_kernel(h_hbm, g_ref, wg_ref, wu_ref, wd_ref, gf_ref, *rest, down_n, final_norm, n_casts):
    cast_in, o_ref, cast_out = rest[:n_casts], rest[n_casts], rest[n_casts + 1:2 * n_casts + 1]
    hn_ref, hbuf_ref, h_sem = rest[2 * n_casts + 1:]
    i, j = pl.program_id(0), pl.program_id(1)
    tm = hbuf_ref.shape[0]

    def h_copy(tile):
        rows = pl.ds(pl.multiple_of(tile * tm, tm), tm)
        return pltpu.make_async_copy(h_hbm.at[rows, :], hbuf_ref, h_sem)

    for src, dst in zip(cast_in, cast_out):
        dst[...] = src[...].astype(BF16)

    @pl.when(jnp.logical_and(i == 0, j == 0))
    def _():
        h_copy(0).start()

    @pl.when(j == 0)
    def _():
        h_copy(i).wait()
        x = hbuf_ref[...]
        hn_ref[...] = _rmsnorm(x, g_ref[...]).astype(BF16)
        o_ref[...] = x

    @pl.when(jnp.logical_and(j == 1, i + 1 < pl.num_programs(0)))
    def _():
        h_copy(i + 1).start()

    hn = hn_ref[...]
    tf = wg_ref.shape[1]
    act = []
    for c in range(0, tf, V7X_MXU_N):
        cols = slice(c, c + V7X_MXU_N)
        gate = _dot(hn, wg_ref[:, cols].astype(BF16))
        up = _dot(hn, wu_ref[:, cols].astype(BF16))
        act.append(((0.5 * jax.nn.silu(gate)) * up).astype(BF16))
    act = act[0] if len(act) == 1 else jnp.concatenate(act, axis=1)
    for n in range(0, o_ref.shape[1], down_n):
        cols = slice(n, n + down_n)
        o_ref[:, cols] += _dot(act, wd_ref[:, cols].astype(BF16))

    if final_norm:
        @pl.when(j == pl.num_programs(1) - 1)
        def _():
            o_ref[...] = _rmsnorm(o_ref[...], gf_ref[...])


def _ffn(h, g, wg, wu, wd, g_final, casts=(), *, final_norm, tm=1024, tf=512, down_n=512):
    s, d = h.shape
    f = wg.shape[1]
    n_i, n_j = s // tm, f // tf
    band = lambda i, j: (jnp.minimum(i * n_j + j, -(-d // CAST_ROWS) - 1), 0)
    assert all(w.shape[0] == d and d % CAST_ROWS == 0 and d // CAST_ROWS <= n_i * n_j for w in casts)
    cast_specs = lambda: [pl.BlockSpec((CAST_ROWS, w.shape[1]), band) for w in casts]
    limit = _vmem_limit(
        _nbytes((tm, d), F32), 2 * _nbytes((tm, d), F32), _nbytes((tm, d), BF16),
        2 * 3 * _nbytes((d, tf), F32), 3 * _nbytes((d, tf), BF16),
        4 * _nbytes((tm, V7X_MXU_N), F32), _nbytes((tm, tf), BF16),
        2 * _nbytes((tm, down_n), F32), _nbytes((tm, d), F32),
        *[3 * _nbytes((CAST_ROWS, w.shape[1]), F32) for w in casts])
    outs = pl.pallas_call(
        functools.partial(_ffn_kernel, down_n=down_n, final_norm=final_norm, n_casts=len(casts)),
        grid=(n_i, n_j),
        in_specs=[
            pl.BlockSpec(memory_space=pl.ANY),
            pl.BlockSpec((1, d), lambda i, j: (0, 0)),
            pl.BlockSpec((d, tf), lambda i, j: (0, j)),
            pl.BlockSpec((d, tf), lambda i, j: (0, j)),
            pl.BlockSpec((tf, d), lambda i, j: (j, 0)),
            pl.BlockSpec((1, d), lambda i, j: (0, 0)),
            *cast_specs(),
        ],
        out_specs=[pl.BlockSpec((tm, d), lambda i, j: (i, 0)), *cast_specs()],
        out_shape=[jax.ShapeDtypeStruct((s, d), F32),
                   *[jax.ShapeDtypeStruct(w.shape, BF16) for w in casts]],
        scratch_shapes=[pltpu.VMEM((tm, d), BF16), pltpu.VMEM((tm, d), F32),
                        pltpu.SemaphoreType.DMA(())],
        compiler_params=pltpu.CompilerParams(dimension_semantics=_ARB2, vmem_limit_bytes=limit),
        name="ffn",
    )(h, g, wg, wu, wd, g_final, *casts)
    return outs


def _in_proj_kernel(h_ref, g_ref, w_hbm, u_ref, v_ref, q_ref, kv_ref, w_ref, w_sem):
    _load_resident([(w_hbm, w_ref, w_sem)])
    hn = _rmsnorm(h_ref[...], g_ref[...]).astype(BF16)
    d_a = u_ref.shape[1]
    d_b = q_ref.shape[1]

    def proj(lo):
        return _dot(hn, w_ref[:, lo:lo + V7X_MXU_N])

    for c in range(0, d_a, V7X_MXU_N):
        u_ref[:, c:c + V7X_MXU_N] = _gelu(proj(c)).astype(BF16)
    for c in range(0, d_a, V7X_MXU_N):
        v_ref[:, c:c + V7X_MXU_N] = _gelu(proj(d_a + c)).astype(BF16)
    for c in range(0, d_b, V7X_MXU_N):
        q_ref[:, c:c + V7X_MXU_N] = proj(2 * d_a + c).astype(BF16)
    for c in range(0, kv_ref.shape[1], V7X_MXU_N):
        kv_ref[:, c:c + V7X_MXU_N] = proj(2 * d_a + d_b + c).astype(BF16)


def _in_proj(h, g, w, d_a, d_b, d_kv, *, tm=1024):
    s, d = h.shape
    n = w.shape[1]
    limit = _vmem_limit(
        2 * _nbytes((tm, d), F32), _nbytes((d, n), BF16), 2 * _nbytes((tm, n), BF16),
        _nbytes((tm, d), F32), _nbytes((tm, d), BF16), 6 * _nbytes((tm, V7X_MXU_N), F32))
    return pl.pallas_call(
        _in_proj_kernel,
        grid=(s // tm,),
        in_specs=[
            pl.BlockSpec((tm, d), lambda i: (i, 0)),
            pl.BlockSpec((1, d), lambda i: (0, 0)),
            _HBM,
        ],
        out_specs=[
            pl.BlockSpec((tm, d_a), lambda i: (i, 0)),
            pl.BlockSpec((tm, d_a), lambda i: (i, 0)),
            pl.BlockSpec((tm, d_b), lambda i: (i, 0)),
            pl.BlockSpec((tm, d_kv), lambda i: (i, 0)),
        ],
        out_shape=[
            jax.ShapeDtypeStruct((s, d_a), BF16),
            jax.ShapeDtypeStruct((s, d_a), BF16),
            jax.ShapeDtypeStruct((s, d_b), BF16),
            jax.ShapeDtypeStruct((s, d_kv), BF16),
        ],
        scratch_shapes=[pltpu.VMEM((d, n), BF16), pltpu.SemaphoreType.DMA(())],
        compiler_params=pltpu.CompilerParams(dimension_semantics=_ARB1, vmem_limit_bytes=limit),
        name="in_proj",
    )(h, g, w)


def _mixer_kernel(sinks_ref, u_ref, v_ref, q_ref, kvc_ref, kvp_ref, h_ref, gv_ref, ws_ref, bst_ref,
                  ga_ref, gb_ref, wout_hbm, o_ref, kvbuf_ref, wsm_ref, y_ref, wout_ref, wout_sem):
    tm, d_a = u_ref.shape
    d_b = q_ref.shape[1]
    n_blocks = tm // CHUNK
    group = d_b // HEAD_DIM // B_KV_HEADS
    kv_w = B_KV_HEADS * HEAD_DIM
    step = pl.program_id(0)
    tile = jnp.minimum(step, pl.num_programs(0) - 2)
    y_new = y_ref.at[step % 2]
    y_old = y_ref.at[(step + 1) % 2]
    _load_resident([(wout_hbm, wout_ref, wout_sem)])

    @pl.when(step == 0)
    def _():
        y_ref[1] = jnp.zeros(y_ref.shape[1:], BF16)
        t_idx = lax.broadcasted_iota(jnp.int32, (CHUNK, CHUNK), 0)
        s_idx = lax.broadcasted_iota(jnp.int32, (CHUNK, CHUNK), 1)
        for g in range(A_GROUPS):
            wsm_ref[g] = jnp.where(s_idx <= t_idx, ws_ref[g], 0.0).astype(BF16)

    kvbuf_ref[0:CHUNK, :] = kvp_ref[...]
    kvbuf_ref[CHUNK:, :] = kvc_ref[...]

    low_half = lax.broadcasted_iota(jnp.int32, (CHUNK, V7X_LANES), 1) < HEAD_DIM
    low2 = lax.broadcasted_iota(jnp.int32, (2 * CHUNK, V7X_LANES), 1) < HEAD_DIM
    top_half = lax.broadcasted_iota(jnp.int32, (V7X_LANES, CHUNK), 0) < HEAD_DIM
    k_pos = lax.broadcasted_iota(jnp.int32, (2 * CHUNK, CHUNK), 0)
    q_pos = lax.broadcasted_iota(jnp.int32, (2 * CHUNK, CHUNK), 1)
    band = (k_pos > q_pos) & (k_pos <= q_pos + CHUNK)

    pieces = [(r, c) for c in range(0, o_ref.shape[1], PROJ_N) for r in range(0, tm, PROJ_M)]
    n_pieces, n_stages = len(pieces), n_blocks * (1 + B_KV_HEADS * (group // 2))
    stage = [0]

    def project():
        stage[0] += 1
        while n_pieces - len(pieces) < stage[0] * n_pieces // n_stages:
            r, c = pieces.pop(0)
            rows, cols = slice(r, r + PROJ_M), slice(c, c + PROJ_N)
            o_ref[rows, cols] = h_ref[rows, cols] + _dot(y_old[rows, :], wout_ref[:, cols])

    for b in range(n_blocks):
        r0 = b * CHUNK
        blk = slice(r0, r0 + CHUNK)

        project()
        vn = _rmsnorm(v_ref[blk, :].astype(F32), gv_ref[...]).astype(BF16)
        ub = u_ref[blk, :].astype(F32)
        ya = []
        for g in range(A_GROUPS):
            cols = slice(g * CHUNK, (g + 1) * CHUNK)
            sv = _dot(wsm_ref[g], vn[:, cols]) + bst_ref[:, g:g + 1]
            ya.append(ub[:, cols] * sv)
        ya = jnp.concatenate(ya, axis=1)
        y_new[blk, 0:d_a] = _rmsnorm(ya, ga_ref[...]).astype(BF16)

        qb = q_ref[blk, :] * (HEAD_DIM ** -0.5)
        kk = kvbuf_ref[r0:r0 + 2 * CHUNK, :]
        k_f = kk[:, 0:kv_w].astype(F32)
        v_f = kk[:, kv_w:2 * kv_w].astype(F32)
        k_rot = pltpu.roll(k_f, HEAD_DIM, axis=1)
        v_rot = pltpu.roll(v_f, HEAD_DIM, axis=1)
        mask = band & (k_pos >= jnp.where(tile == 0, CHUNK, 0)) if b == 0 else band
        ybt = []
        for j in range(B_KV_HEADS):
            if j == 0:
                k2 = jnp.where(low2, k_f, k_rot).astype(BF16)
                v2t = jnp.where(low2, v_f, v_rot).T.astype(BF16)
            else:
                k2 = jnp.where(low2, k_rot, k_f).astype(BF16)
                v2t = jnp.where(low2, v_rot, v_f).T.astype(BF16)
            qm = []
            for hh in range(group):
                head = j * group + hh
                c0 = (head // 2) * V7X_LANES
                qc = qb[:, c0:c0 + V7X_LANES]
                keep = low_half if head % 2 == 0 else jnp.logical_not(low_half)
                qm.append(jnp.where(keep, qc, jnp.zeros_like(qc)))
            st = _dot_nt(k2, jnp.concatenate(qm, axis=0))
            pt = []
            for hh in range(group):
                sink = sinks_ref[j * group + hh]
                sc = jnp.where(mask, st[:, hh * CHUNK:(hh + 1) * CHUNK], NEG)
                m = jnp.maximum(jnp.max(sc, axis=0, keepdims=True), sink)
                e = jnp.exp(sc - m)
                denom = jnp.sum(e, axis=0, keepdims=True) + jnp.exp(sink - m)
                pt.append((e * (1.0 / denom)).astype(BF16))
                if hh % 2 == 1:
                    project()
            ot = _dot(v2t, jnp.concatenate(pt, axis=1))
            for pair in range(group // 2):
                even = ot[:, (2 * pair) * CHUNK:(2 * pair + 1) * CHUNK]
                odd = ot[:, (2 * pair + 1) * CHUNK:(2 * pair + 2) * CHUNK]
                ybt.append(jnp.where(top_half, even, odd))
        ybt = jnp.concatenate(ybt, axis=0)
        ybt = ybt * lax.rsqrt(jnp.mean(ybt * ybt, axis=0, keepdims=True) + EPS)
        y_new[blk, d_a:d_a + d_b] = (ybt.T * gb_ref[...]).astype(BF16)
    assert not pieces


def _mixer(sinks, u, v, q, kv, h, g_v, w_s, b_s_t, g_a, g_b, w_out, *, tm=512):
    s, d = h.shape
    d_a = u.shape[1]
    d_b = q.shape[1]
    d_kv = kv.shape[1]
    d_mix = d_a + d_b
    n_blocks = tm // CHUNK
    n_tiles = s // tm
    limit = _vmem_limit(
        2 * 2 * _nbytes((tm, d_a), BF16), 2 * _nbytes((tm, d_b), BF16),
        2 * _nbytes((tm + CHUNK, d_kv), BF16), 4 * _nbytes((tm, d), F32),
        _nbytes((d_mix, d), BF16), 2 * _nbytes((tm, d_mix), BF16),
        _nbytes((tm + CHUNK, d_kv), BF16), 2 * _nbytes((A_GROUPS, CHUNK, CHUNK), F32),
        2 * _nbytes((tm, d // n_blocks), F32), 32 * _nbytes((CHUNK, d_a), F32))
    vec = lambda n: pl.BlockSpec((1, n), lambda i: (0, 0))
    mixed = lambda i: (jnp.minimum(i, n_tiles - 1), 0)
    projected = lambda i: (jnp.maximum(i - 1, 0), 0)
    prev_block = lambda i: (jnp.maximum(jnp.minimum(i, n_tiles - 1) * n_blocks - 1, 0), 0)
    return pl.pallas_call(
        _mixer_kernel,
        grid=(n_tiles + 1,),
        in_specs=[
            pl.BlockSpec(memory_space=pltpu.SMEM),
            pl.BlockSpec((tm, d_a), mixed),
            pl.BlockSpec((tm, d_a), mixed),
            pl.BlockSpec((tm, d_b), mixed),
            pl.BlockSpec((tm, d_kv), mixed),
            pl.BlockSpec((CHUNK, d_kv), prev_block),
            pl.BlockSpec((tm, d), projected),
            vec(d_a),
            pl.BlockSpec((A_GROUPS, CHUNK, CHUNK), lambda i: (0, 0, 0)),
            pl.BlockSpec((CHUNK, A_GROUPS), lambda i: (0, 0)),
            vec(d_a),
            vec(d_b),
            _HBM,
        ],
        out_specs=pl.BlockSpec((tm, d), projected),
        out_shape=jax.ShapeDtypeStruct((s, d), F32),
        scratch_shapes=[
            pltpu.VMEM((tm + CHUNK, d_kv), BF16),
            pltpu.VMEM((A_GROUPS, CHUNK, CHUNK), BF16),
            pltpu.VMEM((2, tm, d_mix), BF16),
            pltpu.VMEM((d_mix, d), BF16),
            pltpu.SemaphoreType.DMA(()),
        ],
        compiler_params=pltpu.CompilerParams(dimension_semantics=_ARB1, vmem_limit_bytes=limit),
        name="mixer",
    )(sinks, u, v, q, kv, kv, h, g_v, w_s, b_s_t, g_a, g_b, w_out)


def _mem_kv_kernel(mem_ref, g_ref, w_ref, kv_ref):
    mn = _rmsnorm(mem_ref[...], g_ref[...]).astype(BF16)
    kv_ref[...] = _dot(mn, w_ref[...].astype(BF16)).astype(BF16)


def _mem_kv(mem, g, w, *, tn=512):
    m, d = mem.shape
    n = w.shape[1]
    limit = _vmem_limit(
        2 * _nbytes((m, d), F32), 2 * _nbytes((d, tn), F32), _nbytes((d, tn), BF16),
        2 * _nbytes((m, tn), BF16), 3 * _nbytes((m, d), F32), 2 * _nbytes((m, tn), F32))
    return pl.pallas_call(
        _mem_kv_kernel,
        grid=(n // tn,),
        in_specs=[
            pl.BlockSpec((m, d), lambda j: (0, 0)),
            pl.BlockSpec((1, d), lambda j: (0, 0)),
            pl.BlockSpec((d, tn), lambda j: (0, j)),
        ],
        out_specs=pl.BlockSpec((m, tn), lambda j: (0, j)),
        out_shape=jax.ShapeDtypeStruct((m, n), BF16),
        compiler_params=pltpu.CompilerParams(dimension_semantics=_ARB1, vmem_limit_bytes=limit),
        name="mem_kv",
    )(mem, g, w)


def _absorb_kernel(wq_ref, k_ref, v_ref, wo_ref, qk_ref, vo_ref):
    qk_ref[...] = _dot_nt(wq_ref[...], k_ref[...]).astype(BF16)
    vo_ref[...] = _dot(v_ref[...], wo_ref[...]).astype(BF16)


def _absorb(wq, kv, wo):
    d = wq.shape[0]
    m = kv.shape[0]
    hd = d // X_HEADS
    limit = _vmem_limit(
        2 * 2 * _nbytes((d, hd), BF16), 2 * 2 * _nbytes((m, hd), BF16),
        2 * 2 * _nbytes((d, m), BF16), 2 * _nbytes((d, m), F32))
    return pl.pallas_call(
        _absorb_kernel,
        grid=(X_HEADS,),
        in_specs=[
            pl.BlockSpec((d, hd), lambda h: (0, h)),
            pl.BlockSpec((m, hd), lambda h: (0, h)),
            pl.BlockSpec((m, hd), lambda h: (0, X_HEADS + h)),
            pl.BlockSpec((hd, d), lambda h: (h, 0)),
        ],
        out_specs=[pl.BlockSpec((d, m), lambda h: (0, h)), pl.BlockSpec((m, d), lambda h: (h, 0))],
        out_shape=[jax.ShapeDtypeStruct((d, X_HEADS * m), BF16),
                   jax.ShapeDtypeStruct((X_HEADS * m, d), BF16)],
        compiler_params=pltpu.CompilerParams(dimension_semantics=_ARB1, vmem_limit_bytes=limit),
        name="absorb",
    )(wq, kv, kv, wo)


def _cross_kernel(h_ref, g_ref, qk_ref, vo_ref, o_ref):
    x = h_ref[...]
    m = qk_ref.shape[1] // X_HEADS
    scale = (x.shape[1] // X_HEADS) ** -0.5
    hn = _rmsnorm(x, g_ref[...]).astype(BF16)
    probs = []
    for head in range(X_HEADS):
        sc = _dot(hn, qk_ref[:, head * m:(head + 1) * m]) * scale
        mx = jnp.max(sc, axis=1, keepdims=True)
        e = jnp.exp(sc - mx)
        probs.append((e * (1.0 / jnp.sum(e, axis=1, keepdims=True))).astype(BF16))
    o_ref[...] = x + _dot(jnp.concatenate(probs, axis=1), vo_ref[...])


def _cross(h, g, qk, vo, *, tm=512):
    s, d = h.shape
    n = qk.shape[1]
    limit = _vmem_limit(
        4 * _nbytes((tm, d), F32), 2 * 2 * _nbytes((d, n), BF16),
        3 * _nbytes((tm, d), F32), _nbytes((tm, d), BF16), 6 * _nbytes((tm, n), F32))
    return pl.pallas_call(
        _cross_kernel,
        grid=(s // tm,),
        in_specs=[
            pl.BlockSpec((tm, d), lambda i: (i, 0)),
            pl.BlockSpec((1, d), lambda i: (0, 0)),
            pl.BlockSpec((d, n), lambda i: (0, 0)),
            pl.BlockSpec((n, d), lambda i: (0, 0)),
        ],
        out_specs=pl.BlockSpec((tm, d), lambda i: (i, 0)),
        out_shape=jax.ShapeDtypeStruct((s, d), F32),
        compiler_params=pltpu.CompilerParams(dimension_semantics=_ARB1, vmem_limit_bytes=limit),
        name="cross_attn",
    )(h, g, qk, vo)


def _row(v):
    return v.reshape(1, -1)


def kernel(x, mem, g_ffn1, w1_gate, w1_up, w1_down, g_mix, w_in, g_v, w_s, b_s, sinks, g_a_out,
           g_b_out, w_out, g_x, g_mem, w_xq, w_xkv, w_xo, g_ffn2, w2_gate, w2_up, w2_down, g_final):
    depth = g_ffn1.shape[0]
    d_a = g_v.shape[1]
    d_b = g_b_out.shape[1]
    d_kv = 2 * B_KV_HEADS * HEAD_DIM
    outs = []
    bf16_weights = {}
    for b in range(x.shape[0]):
        h = x[b]
        for l in range(depth):
            casts = () if l in bf16_weights else (w_in[l], w_out[l], w_xq[l], w_xo[l])
            h, *copies = _ffn(h, _row(g_ffn1[l]), w1_gate[l], w1_up[l], w1_down[l], _row(g_final),
                              casts, final_norm=False)
            w_in16, w_out16, w_xq16, w_xo16 = bf16_weights.setdefault(l, copies)

            u, v, q, kv = _in_proj(h, _row(g_mix[l]), w_in16, d_a, d_b, d_kv)
            h = _mixer(sinks[l], u, v, q, kv, h, _row(g_v[l]), w_s[l], jnp.transpose(b_s[l]),
                       _row(g_a_out[l]), _row(g_b_out[l]), w_out16)

            mem_kv = _mem_kv(mem[b], _row(g_mem[l]), w_xkv[l])
            qk, vo = _absorb(w_xq16, mem_kv, w_xo16)
            h = _cross(h, _row(g_x[l]), qk, vo)

            h, = _ffn(h, _row(g_ffn2[l]), w2_gate[l], w2_up[l], w2_down[l], _row(g_final),
                      final_norm=(l == depth - 1))
        outs.append(h)
    return outs[0][None] if len(outs) == 1 else jnp.stack(outs, axis=0)
```

```python
import functools

import jax
import jax.numpy as jnp
from jax import lax
from jax.experimental import pallas as pl
from jax.experimental.pallas import tpu as pltpu

F32 = jnp.float32
BF16 = jnp.bfloat16

EPS = 1e-5
NEG = -1e30
CHUNK = 128
A_GROUPS = 8
HEAD_DIM = 64
B_KV_HEADS = 2
X_HEADS = 4

V7X_VMEM_BYTES = 64 * 2**20
V7X_LANES = 128
V7X_MXU_N = 256
CAST_ROWS = 32
PROJ_M = 256
PROJ_N = 256


def _vmem_limit(*byte_counts):
    return int(min(sum(byte_counts), V7X_VMEM_BYTES))


def _nbytes(shape, dtype):
    n = 1
    for s in shape:
        n *= s
    return n * jnp.dtype(dtype).itemsize


def _rmsnorm(x, g):
    ms = jnp.mean(x * x, axis=-1, keepdims=True)
    return x * lax.rsqrt(ms + EPS) * g


def _gelu(x):
    return 0.5 * x * (1.0 + lax.erf(x * (2.0 ** -0.5)))


def _dot(a, b):
    return jnp.dot(a, b, preferred_element_type=F32)


def _dot_nt(a, b):
    return lax.dot_general(a, b, (((1,), (1,)), ((), ())), preferred_element_type=F32)


def _load_resident(pairs):
    copies = [pltpu.make_async_copy(src, dst, sem) for src, dst, sem in pairs]

    @pl.when(pl.program_id(0) == 0)
    def _():
        for cp in copies:
            cp.start()
        for cp in copies:
            cp.wait()


_HBM = pl.BlockSpec(memory_space=pl.ANY)
_ARB1 = ("arbitrary",)
_ARB2 = ("arbitrary", "arbitrary")


def _ffn_kernel(h_hbm, g_ref, wg_ref, wu_ref, wd_ref, gf_ref, *rest, down_n, final_norm, n_casts):
    cast_in, o_ref, cast_out = rest[:n_casts], rest[n_casts], rest[n_casts + 1:2 * n_casts + 1]
    hn_ref, hbuf_ref, h_sem = rest[2 * n_casts + 1:]
    i, j = pl.program_id(0), pl.program_id(1)
    tm = hbuf_ref.shape[0]

    def h_copy(tile):
        rows = pl.ds(pl.multiple_of(tile * tm, tm), tm)
        return pltpu.make_async_copy(h_hbm.at[rows, :], hbuf_ref, h_sem)

    for src, dst in zip(cast_in, cast_out):
        dst[...] = src[...].astype(BF16)

    @pl.when(jnp.logical_and(i == 0, j == 0))
    def _():
        h_copy(0).start()

    @pl.when(j == 0)
    def _():
        h_copy(i).wait()
        x = hbuf_ref[...]
        hn_ref[...] = _rmsnorm(x, g_ref[...]).astype(BF16)
        o_ref[...] = x

    @pl.when(jnp.logical_and(j == 1, i + 1 < pl.num_programs(0)))
    def _():
        h_copy(i + 1).start()

    hn = hn_ref[...]
    tf = wg_ref.shape[1]
    act = []
    for c in range(0, tf, V7X_MXU_N):
        cols = slice(c, c + V7X_MXU_N)
        gate = _dot(hn, wg_ref[:, cols].astype(BF16))
        up = _dot(hn, wu_ref[:, cols].astype(BF16))
        act.append(((0.5 * jax.nn.silu(gate)) * up).astype(BF16))
    act = act[0] if len(act) == 1 else jnp.concatenate(act, axis=1)
    for n in range(0, o_ref.shape[1], down_n):
        cols = slice(n, n + down_n)
        o_ref[:, cols] += _dot(act, wd_ref[:, cols].astype(BF16))

    if final_norm:
        @pl.when(j == pl.num_programs(1) - 1)
        def _():
            o_ref[...] = _rmsnorm(o_ref[...], gf_ref[...])


def _ffn(h, g, wg, wu, wd, g_final, casts=(), *, final_norm, tm=1024, tf=512, down_n=512):
    s, d = h.shape
    f = wg.shape[1]
    n_i, n_j = s // tm, f // tf
    band = lambda i, j: (jnp.minimum(i * n_j + j, -(-d // CAST_ROWS) - 1), 0)
    assert all(w.shape[0] == d and d % CAST_ROWS == 0 and d // CAST_ROWS <= n_i * n_j for w in casts)
    cast_specs = lambda: [pl.BlockSpec((CAST_ROWS, w.shape[1]), band) for w in casts]
    limit = _vmem_limit(
        _nbytes((tm, d), F32), 2 * _nbytes((tm, d), F32), _nbytes((tm, d), BF16),
        2 * 3 * _nbytes((d, tf), F32), 3 * _nbytes((d, tf), BF16),
        4 * _nbytes((tm, V7X_MXU_N), F32), _nbytes((tm, tf), BF16),
        2 * _nbytes((tm, down_n), F32), _nbytes((tm, d), F32),
        *[3 * _nbytes((CAST_ROWS, w.shape[1]), F32) for w in casts])
    outs = pl.pallas_call(
        functools.partial(_ffn_kernel, down_n=down_n, final_norm=final_norm, n_casts=len(casts)),
        grid=(n_i, n_j),
        in_specs=[
            pl.BlockSpec(memory_space=pl.ANY),
            pl.BlockSpec((1, d), lambda i, j: (0, 0)),
            pl.BlockSpec((d, tf), lambda i, j: (0, j)),
            pl.BlockSpec((d, tf), lambda i, j: (0, j)),
            pl.BlockSpec((tf, d), lambda i, j: (j, 0)),
            pl.BlockSpec((1, d), lambda i, j: (0, 0)),
            *cast_specs(),
        ],
        out_specs=[pl.BlockSpec((tm, d), lambda i, j: (i, 0)), *cast_specs()],
        out_shape=[jax.ShapeDtypeStruct((s, d), F32),
                   *[jax.ShapeDtypeStruct(w.shape, BF16) for w in casts]],
        scratch_shapes=[pltpu.VMEM((tm, d), BF16), pltpu.VMEM((tm, d), F32),
                        pltpu.SemaphoreType.DMA(())],
        compiler_params=pltpu.CompilerParams(dimension_semantics=_ARB2, vmem_limit_bytes=limit),
        name="ffn",
    )(h, g, wg, wu, wd, g_final, *casts)
    return outs


def _in_proj_kernel(h_ref, g_ref, w_hbm, u_ref, v_ref, q_ref, kv_ref, w_ref, w_sem):
    _load_resident([(w_hbm, w_ref, w_sem)])
    hn = _rmsnorm(h_ref[...], g_ref[...]).astype(BF16)
    d_a = u_ref.shape[1]
    d_b = q_ref.shape[1]

    def proj(lo):
        return _dot(hn, w_ref[:, lo:lo + V7X_MXU_N])

    for c in range(0, d_a, V7X_MXU_N):
        u_ref[:, c:c + V7X_MXU_N] = _gelu(proj(c)).astype(BF16)
    for c in range(0, d_a, V7X_MXU_N):
        v_ref[:, c:c + V7X_MXU_N] = _gelu(proj(d_a + c)).astype(BF16)
    for c in range(0, d_b, V7X_MXU_N):
        q_ref[:, c:c + V7X_MXU_N] = proj(2 * d_a + c).astype(BF16)
    for c in range(0, kv_ref.shape[1], V7X_MXU_N):
        kv_ref[:, c:c + V7X_MXU_N] = proj(2 * d_a + d_b + c).astype(BF16)


def _in_proj(h, g, w, d_a, d_b, d_kv, *, tm=1024):
    s, d = h.shape
    n = w.shape[1]
    limit = _vmem_limit(
        2 * _nbytes((tm, d), F32), _nbytes((d, n), BF16), 2 * _nbytes((tm, n), BF16),
        _nbytes((tm, d), F32), _nbytes((tm, d), BF16), 6 * _nbytes((tm, V7X_MXU_N), F32))
    return pl.pallas_call(
        _in_proj_kernel,
        grid=(s // tm,),
        in_specs=[
            pl.BlockSpec((tm, d), lambda i: (i, 0)),
            pl.BlockSpec((1, d), lambda i: (0, 0)),
            _HBM,
        ],
        out_specs=[
            pl.BlockSpec((tm, d_a), lambda i: (i, 0)),
            pl.BlockSpec((tm, d_a), lambda i: (i, 0)),
            pl.BlockSpec((tm, d_b), lambda i: (i, 0)),
            pl.BlockSpec((tm, d_kv), lambda i: (i, 0)),
        ],
        out_shape=[
            jax.ShapeDtypeStruct((s, d_a), BF16),
            jax.ShapeDtypeStruct((s, d_a), BF16),
            jax.ShapeDtypeStruct((s, d_b), BF16),
            jax.ShapeDtypeStruct((s, d_kv), BF16),
        ],
        scratch_shapes=[pltpu.VMEM((d, n), BF16), pltpu.SemaphoreType.DMA(())],
        compiler_params=pltpu.CompilerParams(dimension_semantics=_ARB1, vmem_limit_bytes=limit),
        name="in_proj",
    )(h, g, w)


def _mixer_kernel(sinks_ref, u_ref, v_ref, q_ref, kvc_ref, kvp_ref, h_ref, gv_ref, ws_ref, bst_ref,
                  ga_ref, gb_ref, wout_hbm, o_ref, kvbuf_ref, wsm_ref, y_ref, wout_ref, wout_sem):
    tm, d_a = u_ref.shape
    d_b = q_ref.shape[1]
    n_blocks = tm // CHUNK
    group = d_b // HEAD_DIM // B_KV_HEADS
    kv_w = B_KV_HEADS * HEAD_DIM
    step = pl.program_id(0)
    tile = jnp.minimum(step, pl.num_programs(0) - 2)
    y_new = y_ref.at[step % 2]
    y_old = y_ref.at[(step + 1) % 2]
    _load_resident([(wout_hbm, wout_ref, wout_sem)])

    @pl.when(step == 0)
    def _():
        y_ref[1] = jnp.zeros(y_ref.shape[1:], BF16)
        t_idx = lax.broadcasted_iota(jnp.int32, (CHUNK, CHUNK), 0)
        s_idx = lax.broadcasted_iota(jnp.int32, (CHUNK, CHUNK), 1)
        for g in range(A_GROUPS):
            wsm_ref[g] = jnp.where(s_idx <= t_idx, ws_ref[g], 0.0).astype(BF16)

    kvbuf_ref[0:CHUNK, :] = kvp_ref[...]
    kvbuf_ref[CHUNK:, :] = kvc_ref[...]

    low_half = lax.broadcasted_iota(jnp.int32, (CHUNK, V7X_LANES), 1) < HEAD_DIM
    low2 = lax.broadcasted_iota(jnp.int32, (2 * CHUNK, V7X_LANES), 1) < HEAD_DIM
    top_half = lax.broadcasted_iota(jnp.int32, (V7X_LANES, CHUNK), 0) < HEAD_DIM
    k_pos = lax.broadcasted_iota(jnp.int32, (2 * CHUNK, CHUNK), 0)
    q_pos = lax.broadcasted_iota(jnp.int32, (2 * CHUNK, CHUNK), 1)
    band = (k_pos > q_pos) & (k_pos <= q_pos + CHUNK)

    pieces = [(r, c) for c in range(0, o_ref.shape[1], PROJ_N) for r in range(0, tm, PROJ_M)]
    n_pieces, n_stages = len(pieces), n_blocks * (1 + B_KV_HEADS * (group // 2))
    stage = [0]

    def project():
        stage[0] += 1
        while n_pieces - len(pieces) < stage[0] * n_pieces // n_stages:
            r, c = pieces.pop(0)
            rows, cols = slice(r, r + PROJ_M), slice(c, c + PROJ_N)
            o_ref[rows, cols] = h_ref[rows, cols] + _dot(y_old[rows, :], wout_ref[:, cols])

    for b in range(n_blocks):
        r0 = b * CHUNK
        blk = slice(r0, r0 + CHUNK)

        project()
        vn = _rmsnorm(v_ref[blk, :].astype(F32), gv_ref[...]).astype(BF16)
        ub = u_ref[blk, :].astype(F32)
        ya = []
        for g in range(A_GROUPS):
            cols = slice(g * CHUNK, (g + 1) * CHUNK)
            sv = _dot(wsm_ref[g], vn[:, cols]) + bst_ref[:, g:g + 1]
            ya.append(ub[:, cols] * sv)
        ya = jnp.concatenate(ya, axis=1)
        y_new[blk, 0:d_a] = _rmsnorm(ya, ga_ref[...]).astype(BF16)

        qb = q_ref[blk, :] * (HEAD_DIM ** -0.5)
        kk = kvbuf_ref[r0:r0 + 2 * CHUNK, :]
        k_f = kk[:, 0:kv_w].astype(F32)
        v_f = kk[:, kv_w:2 * kv_w].astype(F32)
        k_rot = pltpu.roll(k_f, HEAD_DIM, axis=1)
        v_rot = pltpu.roll(v_f, HEAD_DIM, axis=1)
        mask = band & (k_pos >= jnp.where(tile == 0, CHUNK, 0)) if b == 0 else band
        ybt = []
        for j in range(B_KV_HEADS):
            if j == 0:
                k2 = jnp.where(low2, k_f, k_rot).astype(BF16)
                v2t = jnp.where(low2, v_f, v_rot).T.astype(BF16)
            else:
                k2 = jnp.where(low2, k_rot, k_f).astype(BF16)
                v2t = jnp.where(low2, v_rot, v_f).T.astype(BF16)
            qm = []
            for hh in range(group):
                head = j * group + hh
                c0 = (head // 2) * V7X_LANES
                qc = qb[:, c0:c0 + V7X_LANES]
                keep = low_half if head % 2 == 0 else jnp.logical_not(low_half)
                qm.append(jnp.where(keep, qc, jnp.zeros_like(qc)))
            st = _dot_nt(k2, jnp.concatenate(qm, axis=0))
            pt = []
            for hh in range(group):
                sink = sinks_ref[j * group + hh]
                sc = jnp.where(mask, st[:, hh * CHUNK:(hh + 1) * CHUNK], NEG)
                m = jnp.maximum(jnp.max(sc, axis=0, keepdims=True), sink)
                e = jnp.exp(sc - m)
                denom = jnp.sum(e, axis=0, keepdims=True) + jnp.exp(sink - m)
                pt.append((e * (1.0 / denom)).astype(BF16))
                if hh % 2 == 1:
                    project()
            ot = _dot(v2t, jnp.concatenate(pt, axis=1))
            for pair in range(group // 2):
                even = ot[:, (2 * pair) * CHUNK:(2 * pair + 1) * CHUNK]
                odd = ot[:, (2 * pair + 1) * CHUNK:(2 * pair + 2) * CHUNK]
                ybt.append(jnp.where(top_half, even, odd))
        ybt = jnp.concatenate(ybt, axis=0)
        ybt = ybt * lax.rsqrt(jnp.mean(ybt * ybt, axis=0, keepdims=True) + EPS)
        y_new[blk, d_a:d_a + d_b] = (ybt.T * gb_ref[...]).astype(BF16)
    assert not pieces


def _mixer(sinks, u, v, q, kv, h, g_v, w_s, b_s_t, g_a, g_b, w_out, *, tm=512):
    s, d = h.shape
    d_a = u.shape[1]
    d_b = q.shape[1]
    d_kv = kv.shape[1]
    d_mix = d_a + d_b
    n_blocks = tm // CHUNK
    n_tiles = s // tm
    limit = _vmem_limit(
        2 * 2 * _nbytes((tm, d_a), BF16), 2 * _nbytes((tm, d_b), BF16),
        2 * _nbytes((tm + CHUNK, d_kv), BF16), 4 * _nbytes((tm, d), F32),
        _nbytes((d_mix, d), BF16), 2 * _nbytes((tm, d_mix), BF16),
        _nbytes((tm + CHUNK, d_kv), BF16), 2 * _nbytes((A_GROUPS, CHUNK, CHUNK), F32),
        2 * _nbytes((tm, d // n_blocks), F32), 32 * _nbytes((CHUNK, d_a), F32))
    vec = lambda n: pl.BlockSpec((1, n), lambda i: (0, 0))
    mixed = lambda i: (jnp.minimum(i, n_tiles - 1), 0)
    projected = lambda i: (jnp.maximum(i - 1, 0), 0)
    prev_block = lambda i: (jnp.maximum(jnp.minimum(i, n_tiles - 1) * n_blocks - 1, 0), 0)
    return pl.pallas_call(
        _mixer_kernel,
        grid=(n_tiles + 1,),
        in_specs=[
            pl.BlockSpec(memory_space=pltpu.SMEM),
            pl.BlockSpec((tm, d_a), mixed),
            pl.BlockSpec((tm, d_a), mixed),
            pl.BlockSpec((tm, d_b), mixed),
            pl.BlockSpec((tm, d_kv), mixed),
            pl.BlockSpec((CHUNK, d_kv), prev_block),
            pl.BlockSpec((tm, d), projected),
            vec(d_a),
            pl.BlockSpec((A_GROUPS, CHUNK, CHUNK), lambda i: (0, 0, 0)),
            pl.BlockSpec((CHUNK, A_GROUPS), lambda i: (0, 0)),
            vec(d_a),
            vec(d_b),
            _HBM,
        ],
        out_specs=pl.BlockSpec((tm, d), projected),
        out_shape=jax.ShapeDtypeStruct((s, d), F32),
        scratch_shapes=[
            pltpu.VMEM((tm + CHUNK, d_kv), BF16),
            pltpu.VMEM((A_GROUPS, CHUNK, CHUNK), BF16),
            pltpu.VMEM((2, tm, d_mix), BF16),
            pltpu.VMEM((d_mix, d), BF16),
            pltpu.SemaphoreType.DMA(()),
        ],
        compiler_params=pltpu.CompilerParams(dimension_semantics=_ARB1, vmem_limit_bytes=limit),
        name="mixer",
    )(sinks, u, v, q, kv, kv, h, g_v, w_s, b_s_t, g_a, g_b, w_out)


def _mem_kv_kernel(mem_ref, g_ref, w_ref, kv_ref):
    mn = _rmsnorm(mem_ref[...], g_ref[...]).astype(BF16)
    kv_ref[...] = _dot(mn, w_ref[...].astype(BF16)).astype(BF16)


def _mem_kv(mem, g, w, *, tn=512):
    m, d = mem.shape
    n = w.shape[1]
    limit = V7X_VMEM_BYTES
    return pl.pallas_call(
        _mem_kv_kernel,
        grid=(n // tn,),
        in_specs=[
            pl.BlockSpec((m, d), lambda j: (0, 0)),
            pl.BlockSpec((1, d), lambda j: (0, 0)),
            pl.BlockSpec((d, tn), lambda j: (0, j)),
        ],
        out_specs=pl.BlockSpec((m, tn), lambda j: (0, j)),
        out_shape=jax.ShapeDtypeStruct((m, n), BF16),
        compiler_params=pltpu.CompilerParams(dimension_semantics=_ARB1, vmem_limit_bytes=limit),
        name="mem_kv",
    )(mem, g, w)


def _absorb_kernel(wq_ref, k_ref, v_ref, wo_ref, qk_ref, vo_ref):
    qk_ref[...] = _dot_nt(wq_ref[...], k_ref[...]).astype(BF16)
    vo_ref[...] = _dot(v_ref[...], wo_ref[...]).astype(BF16)


def _absorb(wq, kv, wo):
    d = wq.shape[0]
    m = kv.shape[0]
    hd = d // X_HEADS
    limit = V7X_VMEM_BYTES
    return pl.pallas_call(
        _absorb_kernel,
        grid=(X_HEADS,),
        in_specs=[
            pl.BlockSpec((d, hd), lambda h: (0, h)),
            pl.BlockSpec((m, hd), lambda h: (0, h)),
            pl.BlockSpec((m, hd), lambda h: (0, X_HEADS + h)),
            pl.BlockSpec((hd, d), lambda h: (h, 0)),
        ],
        out_specs=[pl.BlockSpec((d, m), lambda h: (0, h)), pl.BlockSpec((m, d), lambda h: (h, 0))],
        out_shape=[jax.ShapeDtypeStruct((d, X_HEADS * m), BF16),
                   jax.ShapeDtypeStruct((X_HEADS * m, d), BF16)],
        compiler_params=pltpu.CompilerParams(dimension_semantics=_ARB1, vmem_limit_bytes=limit),
        name="absorb",
    )(wq, kv, kv, wo)


def _cross_kernel(h_ref, g_ref, qk_ref, vo_ref, o_ref):
    x = h_ref[...]
    m = qk_ref.shape[1] // X_HEADS
    scale = (x.shape[1] // X_HEADS) ** -0.5
    hn = _rmsnorm(x, g_ref[...]).astype(BF16)
    probs = []
    for head in range(X_HEADS):
        sc = _dot(hn, qk_ref[:, head * m:(head + 1) * m]) * scale
        mx = jnp.max(sc, axis=1, keepdims=True)
        e = jnp.exp(sc - mx)
        probs.append((e * (1.0 / jnp.sum(e, axis=1, keepdims=True))).astype(BF16))
    o_ref[...] = x + _dot(jnp.concatenate(probs, axis=1), vo_ref[...])


def _cross(h, g, qk, vo, *, tm=1024):
    s, d = h.shape
    n = qk.shape[1]
    limit = _vmem_limit(
        4 * _nbytes((tm, d), F32), 2 * 2 * _nbytes((d, n), BF16),
        3 * _nbytes((tm, d), F32), _nbytes((tm, d), BF16), 6 * _nbytes((tm, n), F32))
    return pl.pallas_call(
        _cross_kernel,
        grid=(s // tm,),
        in_specs=[
            pl.BlockSpec((tm, d), lambda i: (i, 0)),
            pl.BlockSpec((1, d), lambda i: (0, 0)),
            pl.BlockSpec((d, n), lambda i: (0, 0)),
            pl.BlockSpec((n, d), lambda i: (0, 0)),
        ],
        out_specs=pl.BlockSpec((tm, d), lambda i: (i, 0)),
        out_shape=jax.ShapeDtypeStruct((s, d), F32),
        compiler_params=pltpu.CompilerParams(dimension_semantics=_ARB1, vmem_limit_bytes=limit),
        name="cross_attn",
    )(h, g, qk, vo)


def _row(v):
    return v.reshape(1, -1)


def kernel(x, mem, g_ffn1, w1_gate, w1_up, w1_down, g_mix, w_in, g_v, w_s, b_s, sinks, g_a_out,
           g_b_out, w_out, g_x, g_mem, w_xq, w_xkv, w_xo, g_ffn2, w2_gate, w2_up, w2_down, g_final):
    depth = g_ffn1.shape[0]
    d_a = g_v.shape[1]
    d_b = g_b_out.shape[1]
    d_kv = 2 * B_KV_HEADS * HEAD_DIM
    outs = []
    bf16_weights = {}
    for b in range(x.shape[0]):
        h = x[b]
        for l in range(depth):
            casts = () if l in bf16_weights else (w_in[l], w_out[l], w_xq[l], w_xo[l])
            h, *copies = _ffn(h, _row(g_ffn1[l]), w1_gate[l], w1_up[l], w1_down[l], _row(g_final),
                              casts, final_norm=False)
            w_in16, w_out16, w_xq16, w_xo16 = bf16_weights.setdefault(l, copies)

            u, v, q, kv = _in_proj(h, _row(g_mix[l]), w_in16, d_a, d_b, d_kv)
            h = _mixer(sinks[l], u, v, q, kv, h, _row(g_v[l]), w_s[l], jnp.transpose(b_s[l]),
                       _row(g_a_out[l]), _row(g_b_out[l]), w_out16)

            mem_kv = _mem_kv(mem[b], _row(g_mem[l]), w_xkv[l])
            qk, vo = _absorb(w_xq16, mem_kv, w_xo16)
            h = _cross(h, _row(g_x[l]), qk, vo)

            h, = _ffn(h, _row(g_ffn2[l]), w2_gate[l], w2_up[l], w2_down[l], _row(g_final),
                      final_norm=(l == depth - 1))
        outs.append(h)
    return outs[0][None] if len(outs) == 1 else jnp.stack(outs, axis=0)
```

```python
import functools

import jax
import jax.numpy as jnp
from jax import lax
from jax.experimental import pallas as pl
from jax.experimental.pallas import tpu as pltpu

F32 = jnp.float32
BF16 = jnp.bfloat16

EPS = 1e-5
NEG = -1e30
CHUNK = 128
A_GROUPS = 8
HEAD_DIM = 64
B_KV_HEADS = 2
X_HEADS = 4

V7X_VMEM_BYTES = 64 * 2**20
V7X_LANES = 128
V7X_MXU_N = 256
CAST_ROWS = 32
PROJ_M = 256
PROJ_N = 256


def _vmem_limit(*byte_counts):
    return int(min(sum(byte_counts), V7X_VMEM_BYTES))


def _nbytes(shape, dtype):
    n = 1
    for s in shape:
        n *= s
    return n * jnp.dtype(dtype).itemsize


def _rmsnorm(x, g):
    ms = jnp.mean(x * x, axis=-1, keepdims=True)
    return x * lax.rsqrt(ms + EPS) * g


def _gelu(x):
    return 0.5 * x * (1.0 + lax.erf(x * (2.0 ** -0.5)))


def _dot(a, b):
    return jnp.dot(a, b, preferred_element_type=F32)


def _dot_nt(a, b):
    return lax.dot_general(a, b, (((1,), (1,)), ((), ())), preferred_element_type=F32)


def _load_resident(pairs):
    copies = [pltpu.make_async_copy(src, dst, sem) for src, dst, sem in pairs]

    @pl.when(pl.program_id(0) == 0)
    def _():
        for cp in copies:
            cp.start()
        for cp in copies:
            cp.wait()


_HBM = pl.BlockSpec(memory_space=pl.ANY)
_ARB1 = ("arbitrary",)
_ARB2 = ("arbitrary", "arbitrary")


def _ffn_kernel(h_hbm, g_ref, wg_ref, wu_ref, wd_ref, gf_ref, *rest, down_n, final_norm, n_casts):
    cast_in, o_ref, cast_out = rest[:n_casts], rest[n_casts], rest[n_casts + 1:2 * n_casts + 1]
    hn_ref, hbuf_ref, h_sem = rest[2 * n_casts + 1:]
    i, j = pl.program_id(0), pl.program_id(1)
    tm = hbuf_ref.shape[0]

    def h_copy(tile):
        rows = pl.ds(pl.multiple_of(tile * tm, tm), tm)
        return pltpu.make_async_copy(h_hbm.at[rows, :], hbuf_ref, h_sem)

    for src, dst in zip(cast_in, cast_out):
        dst[...] = src[...].astype(BF16)

    @pl.when(jnp.logical_and(i == 0, j == 0))
    def _():
        h_copy(0).start()

    @pl.when(j == 0)
    def _():
        h_copy(i).wait()
        x = hbuf_ref[...]
        hn_ref[...] = _rmsnorm(x, g_ref[...]).astype(BF16)
        o_ref[...] = x

    @pl.when(jnp.logical_and(j == 1, i + 1 < pl.num_programs(0)))
    def _():
        h_copy(i + 1).start()

    hn = hn_ref[...]
    tf = wg_ref.shape[1]
    act = []
    for c in range(0, tf, V7X_MXU_N):
        cols = slice(c, c + V7X_MXU_N)
        gate = _dot(hn, wg_ref[:, cols].astype(BF16))
        up = _dot(hn, wu_ref[:, cols].astype(BF16))
        act.append(((0.5 * jax.nn.silu(gate)) * up).astype(BF16))
    act = act[0] if len(act) == 1 else jnp.concatenate(act, axis=1)
    for n in range(0, o_ref.shape[1], down_n):
        cols = slice(n, n + down_n)
        o_ref[:, cols] += _dot(act, wd_ref[:, cols].astype(BF16))

    if final_norm:
        @pl.when(j == pl.num_programs(1) - 1)
        def _():
            o_ref[...] = _rmsnorm(o_ref[...], gf_ref[...])


def _ffn(h, g, wg, wu, wd, g_final, casts=(), *, final_norm, tm=1024, tf=512, down_n=512):
    s, d = h.shape
    f = wg.shape[1]
    n_i, n_j = s // tm, f // tf
    band = lambda i, j: (jnp.minimum(i * n_j + j, -(-d // CAST_ROWS) - 1), 0)
    assert all(w.shape[0] == d and d % CAST_ROWS == 0 and d // CAST_ROWS <= n_i * n_j for w in casts)
    cast_specs = lambda: [pl.BlockSpec((CAST_ROWS, w.shape[1]), band) for w in casts]
    limit = _vmem_limit(
        _nbytes((tm, d), F32), 2 * _nbytes((tm, d), F32), _nbytes((tm, d), BF16),
        2 * 3 * _nbytes((d, tf), F32), 3 * _nbytes((d, tf), BF16),
        4 * _nbytes((tm, V7X_MXU_N), F32), _nbytes((tm, tf), BF16),
        2 * _nbytes((tm, down_n), F32), _nbytes((tm, d), F32),
        *[3 * _nbytes((CAST_ROWS, w.shape[1]), F32) for w in casts])
    outs = pl.pallas_call(
        functools.partial(_ffn_kernel, down_n=down_n, final_norm=final_norm, n_casts=len(casts)),
        grid=(n_i, n_j),
        in_specs=[
            pl.BlockSpec(memory_space=pl.ANY),
            pl.BlockSpec((1, d), lambda i, j: (0, 0)),
            pl.BlockSpec((d, tf), lambda i, j: (0, j)),
            pl.BlockSpec((d, tf), lambda i, j: (0, j)),
            pl.BlockSpec((tf, d), lambda i, j: (j, 0)),
            pl.BlockSpec((1, d), lambda i, j: (0, 0)),
            *cast_specs(),
        ],
        out_specs=[pl.BlockSpec((tm, d), lambda i, j: (i, 0)), *cast_specs()],
        out_shape=[jax.ShapeDtypeStruct((s, d), F32),
                   *[jax.ShapeDtypeStruct(w.shape, BF16) for w in casts]],
        scratch_shapes=[pltpu.VMEM((tm, d), BF16), pltpu.VMEM((tm, d), F32),
                        pltpu.SemaphoreType.DMA(())],
        compiler_params=pltpu.CompilerParams(dimension_semantics=_ARB2, vmem_limit_bytes=limit),
        name="ffn",
    )(h, g, wg, wu, wd, g_final, *casts)
    return outs


def _in_proj_kernel(h_ref, g_ref, w_hbm, u_ref, v_ref, q_ref, kv_ref, w_ref, w_sem):
    _load_resident([(w_hbm, w_ref, w_sem)])
    hn = _rmsnorm(h_ref[...], g_ref[...]).astype(BF16)
    d_a = u_ref.shape[1]
    d_b = q_ref.shape[1]

    def proj(lo):
        return _dot(hn, w_ref[:, lo:lo + V7X_MXU_N])

    for c in range(0, d_a, V7X_MXU_N):
        u_ref[:, c:c + V7X_MXU_N] = _gelu(proj(c)).astype(BF16)
    for c in range(0, d_a, V7X_MXU_N):
        v_ref[:, c:c + V7X_MXU_N] = _gelu(proj(d_a + c)).astype(BF16)
    for c in range(0, d_b, V7X_MXU_N):
        q_ref[:, c:c + V7X_MXU_N] = proj(2 * d_a + c).astype(BF16)
    for c in range(0, kv_ref.shape[1], V7X_MXU_N):
        kv_ref[:, c:c + V7X_MXU_N] = proj(2 * d_a + d_b + c).astype(BF16)


def _in_proj(h, g, w, d_a, d_b, d_kv, *, tm=1024):
    s, d = h.shape
    n = w.shape[1]
    limit = _vmem_limit(
        2 * _nbytes((tm, d), F32), _nbytes((d, n), BF16), 2 * _nbytes((tm, n), BF16),
        _nbytes((tm, d), F32), _nbytes((tm, d), BF16), 6 * _nbytes((tm, V7X_MXU_N), F32))
    return pl.pallas_call(
        _in_proj_kernel,
        grid=(s // tm,),
        in_specs=[
            pl.BlockSpec((tm, d), lambda i: (i, 0)),
            pl.BlockSpec((1, d), lambda i: (0, 0)),
            _HBM,
        ],
        out_specs=[
            pl.BlockSpec((tm, d_a), lambda i: (i, 0)),
            pl.BlockSpec((tm, d_a), lambda i: (i, 0)),
            pl.BlockSpec((tm, d_b), lambda i: (i, 0)),
            pl.BlockSpec((tm, d_kv), lambda i: (i, 0)),
        ],
        out_shape=[
            jax.ShapeDtypeStruct((s, d_a), BF16),
            jax.ShapeDtypeStruct((s, d_a), BF16),
            jax.ShapeDtypeStruct((s, d_b), BF16),
            jax.ShapeDtypeStruct((s, d_kv), BF16),
        ],
        scratch_shapes=[pltpu.VMEM((d, n), BF16), pltpu.SemaphoreType.DMA(())],
        compiler_params=pltpu.CompilerParams(dimension_semantics=_ARB1, vmem_limit_bytes=limit),
        name="in_proj",
    )(h, g, w)


def _mixer_kernel(sinks_ref, u_ref, v_ref, q_ref, kvc_ref, kvp_ref, h_ref, gv_ref, ws_ref, bst_ref,
                  ga_ref, gb_ref, wout_hbm, o_ref, kvbuf_ref, wsm_ref, y_ref, wout_ref, wout_sem):
    tm, d_a = u_ref.shape
    d_b = q_ref.shape[1]
    n_blocks = tm // CHUNK
    group = d_b // HEAD_DIM // B_KV_HEADS
    kv_w = B_KV_HEADS * HEAD_DIM
    step = pl.program_id(0)
    tile = jnp.minimum(step, pl.num_programs(0) - 2)
    y_new = y_ref.at[step % 2]
    y_old = y_ref.at[(step + 1) % 2]
    _load_resident([(wout_hbm, wout_ref, wout_sem)])

    @pl.when(step == 0)
    def _():
        y_ref[1] = jnp.zeros(y_ref.shape[1:], BF16)
        t_idx = lax.broadcasted_iota(jnp.int32, (CHUNK, CHUNK), 0)
        s_idx = lax.broadcasted_iota(jnp.int32, (CHUNK, CHUNK), 1)
        for g in range(A_GROUPS):
            wsm_ref[g] = jnp.where(s_idx <= t_idx, ws_ref[g], 0.0).astype(BF16)

    kvbuf_ref[0:CHUNK, :] = kvp_ref[...]
    kvbuf_ref[CHUNK:, :] = kvc_ref[...]

    low_half = lax.broadcasted_iota(jnp.int32, (CHUNK, V7X_LANES), 1) < HEAD_DIM
    low2 = lax.broadcasted_iota(jnp.int32, (2 * CHUNK, V7X_LANES), 1) < HEAD_DIM
    top_half = lax.broadcasted_iota(jnp.int32, (V7X_LANES, CHUNK), 0) < HEAD_DIM
    k_pos = lax.broadcasted_iota(jnp.int32, (2 * CHUNK, CHUNK), 0)
    q_pos = lax.broadcasted_iota(jnp.int32, (2 * CHUNK, CHUNK), 1)
    band = (k_pos > q_pos) & (k_pos <= q_pos + CHUNK)

    pieces = [(r, c) for c in range(0, o_ref.shape[1], PROJ_N) for r in range(0, tm, PROJ_M)]
    n_pieces, n_stages = len(pieces), n_blocks * (1 + B_KV_HEADS * (group // 2))
    stage = [0]

    def project():
        stage[0] += 1
        while n_pieces - len(pieces) < stage[0] * n_pieces // n_stages:
            r, c = pieces.pop(0)
            rows, cols = slice(r, r + PROJ_M), slice(c, c + PROJ_N)
            o_ref[rows, cols] = h_ref[rows, cols] + _dot(y_old[rows, :], wout_ref[:, cols])

    for b in range(n_blocks):
        r0 = b * CHUNK
        blk = slice(r0, r0 + CHUNK)

        project()
        vn = _rmsnorm(v_ref[blk, :].astype(F32), gv_ref[...]).astype(BF16)
        ub = u_ref[blk, :].astype(F32)
        ya = []
        for g in range(A_GROUPS):
            cols = slice(g * CHUNK, (g + 1) * CHUNK)
            sv = _dot(wsm_ref[g], vn[:, cols]) + bst_ref[:, g:g + 1]
            ya.append(ub[:, cols] * sv)
        ya = jnp.concatenate(ya, axis=1)
        y_new[blk, 0:d_a] = _rmsnorm(ya, ga_ref[...]).astype(BF16)

        qb = q_ref[blk, :] * (HEAD_DIM ** -0.5)
        kk = kvbuf_ref[r0:r0 + 2 * CHUNK, :]
        k_f = kk[:, 0:kv_w].astype(F32)
        v_f = kk[:, kv_w:2 * kv_w].astype(F32)
        k_rot = pltpu.roll(k_f, HEAD_DIM, axis=1)
        v_rot = pltpu.roll(v_f, HEAD_DIM, axis=1)
        mask = band & (k_pos >= jnp.where(tile == 0, CHUNK, 0)) if b == 0 else band
        ybt = []
        for j in range(B_KV_HEADS):
            if j == 0:
                k2 = jnp.where(low2, k_f, k_rot).astype(BF16)
                v2t = jnp.where(low2, v_f, v_rot).T.astype(BF16)
            else:
                k2 = jnp.where(low2, k_rot, k_f).astype(BF16)
                v2t = jnp.where(low2, v_rot, v_f).T.astype(BF16)
            qm = []
            for hh in range(group):
                head = j * group + hh
                c0 = (head // 2) * V7X_LANES
                qc = qb[:, c0:c0 + V7X_LANES]
                keep = low_half if head % 2 == 0 else jnp.logical_not(low_half)
                qm.append(jnp.where(keep, qc, jnp.zeros_like(qc)))
            st = _dot_nt(k2, jnp.concatenate(qm, axis=0))
            pt = []
            for hh in range(group):
                sink = sinks_ref[j * group + hh]
                sc = jnp.where(mask, st[:, hh * CHUNK:(hh + 1) * CHUNK], NEG)
                m = jnp.maximum(jnp.max(sc, axis=0, keepdims=True), sink)
                e = jnp.exp(sc - m)
                denom = jnp.sum(e, axis=0, keepdims=True) + jnp.exp(sink - m)
                pt.append((e * (1.0 / denom)).astype(BF16))
                if hh % 2 == 1:
                    project()
            ot = _dot(v2t, jnp.concatenate(pt, axis=1))
            for pair in range(group // 2):
                even = ot[:, (2 * pair) * CHUNK:(2 * pair + 1) * CHUNK]
                odd = ot[:, (2 * pair + 1) * CHUNK:(2 * pair + 2) * CHUNK]
                ybt.append(jnp.where(top_half, even, odd))
        ybt = jnp.concatenate(ybt, axis=0)
        ybt = ybt * lax.rsqrt(jnp.mean(ybt * ybt, axis=0, keepdims=True) + EPS)
        y_new[blk, d_a:d_a + d_b] = (ybt.T * gb_ref[...]).astype(BF16)
    assert not pieces


def _mixer(sinks, u, v, q, kv, h, g_v, w_s, b_s_t, g_a, g_b, w_out, *, tm=512):
    s, d = h.shape
    d_a = u.shape[1]
    d_b = q.shape[1]
    d_kv = kv.shape[1]
    d_mix = d_a + d_b
    n_blocks = tm // CHUNK
    n_tiles = s // tm
    limit = _vmem_limit(
        2 * 2 * _nbytes((tm, d_a), BF16), 2 * _nbytes((tm, d_b), BF16),
        2 * _nbytes((tm + CHUNK, d_kv), BF16), 4 * _nbytes((tm, d), F32),
        _nbytes((d_mix, d), BF16), 2 * _nbytes((tm, d_mix), BF16),
        _nbytes((tm + CHUNK, d_kv), BF16), 2 * _nbytes((A_GROUPS, CHUNK, CHUNK), F32),
        2 * _nbytes((tm, d // n_blocks), F32), 32 * _nbytes((CHUNK, d_a), F32))
    vec = lambda n: pl.BlockSpec((1, n), lambda i: (0, 0))
    mixed = lambda i: (jnp.minimum(i, n_tiles - 1), 0)
    projected = lambda i: (jnp.maximum(i - 1, 0), 0)
    prev_block = lambda i: (jnp.maximum(jnp.minimum(i, n_tiles - 1) * n_blocks - 1, 0), 0)
    return pl.pallas_call(
        _mixer_kernel,
        grid=(n_tiles + 1,),
        in_specs=[
            pl.BlockSpec(memory_space=pltpu.SMEM),
            pl.BlockSpec((tm, d_a), mixed),
            pl.BlockSpec((tm, d_a), mixed),
            pl.BlockSpec((tm, d_b), mixed),
            pl.BlockSpec((tm, d_kv), mixed),
            pl.BlockSpec((CHUNK, d_kv), prev_block),
            pl.BlockSpec((tm, d), projected),
            vec(d_a),
            pl.BlockSpec((A_GROUPS, CHUNK, CHUNK), lambda i: (0, 0, 0)),
            pl.BlockSpec((CHUNK, A_GROUPS), lambda i: (0, 0)),
            vec(d_a),
            vec(d_b),
            _HBM,
        ],
        out_specs=pl.BlockSpec((tm, d), projected),
        out_shape=jax.ShapeDtypeStruct((s, d), F32),
        scratch_shapes=[
            pltpu.VMEM((tm + CHUNK, d_kv), BF16),
            pltpu.VMEM((A_GROUPS, CHUNK, CHUNK), BF16),
            pltpu.VMEM((2, tm, d_mix), BF16),
            pltpu.VMEM((d_mix, d), BF16),
            pltpu.SemaphoreType.DMA(()),
        ],
        compiler_params=pltpu.CompilerParams(dimension_semantics=_ARB1, vmem_limit_bytes=limit),
        name="mixer",
    )(sinks, u, v, q, kv, kv, h, g_v, w_s, b_s_t, g_a, g_b, w_out)


def _mem_kv_kernel(mem_ref, g_ref, w_ref, kv_ref, mn_ref, acc_ref):
    k = pl.program_id(0)
    tk = w_ref.shape[0]

    @pl.when(k == 0)
    def _():
        mn = _rmsnorm(mem_ref[...], g_ref[...]).astype(BF16)
        for band in range(mn_ref.shape[0]):
            mn_ref[band] = mn[:, band * tk:(band + 1) * tk]
        acc_ref[...] = jnp.zeros(acc_ref.shape, F32)

    acc_ref[...] += _dot(mn_ref[k], w_ref[...].astype(BF16))

    @pl.when(k == pl.num_programs(0) - 1)
    def _():
        kv_ref[...] = acc_ref[...].astype(BF16)


def _mem_kv(mem, g, w, *, tk=256):
    m, d = mem.shape
    n = w.shape[1]
    limit = V7X_VMEM_BYTES
    return pl.pallas_call(
        _mem_kv_kernel,
        grid=(d // tk,),
        in_specs=[
            pl.BlockSpec((m, d), lambda k: (0, 0)),
            pl.BlockSpec((1, d), lambda k: (0, 0)),
            pl.BlockSpec((tk, n), lambda k: (k, 0)),
        ],
        out_specs=pl.BlockSpec((m, n), lambda k: (0, 0)),
        out_shape=jax.ShapeDtypeStruct((m, n), BF16),
        scratch_shapes=[pltpu.VMEM((d // tk, m, tk), BF16), pltpu.VMEM((m, n), F32)],
        compiler_params=pltpu.CompilerParams(dimension_semantics=_ARB1, vmem_limit_bytes=limit),
        name="mem_kv",
    )(mem, g, w)


def _absorb_kernel(wq_ref, k_ref, v_ref, wo_ref, qk_ref, vo_ref):
    m = k_ref.shape[0]
    hd = k_ref.shape[1] // X_HEADS
    for head in range(X_HEADS):
        cols = slice(head * hd, (head + 1) * hd)
        qk_ref[:, head * m:(head + 1) * m] = _dot_nt(wq_ref[:, cols], k_ref[:, cols]).astype(BF16)
    vo_ref[...] = _dot(v_ref[...], wo_ref[...]).astype(BF16)


def _absorb(wq, kv, wo):
    d = wq.shape[0]
    m = kv.shape[0]
    hd = d // X_HEADS
    limit = V7X_VMEM_BYTES
    return pl.pallas_call(
        _absorb_kernel,
        grid=(X_HEADS,),
        in_specs=[
            pl.BlockSpec((d // X_HEADS, d), lambda s: (s, 0)),
            pl.BlockSpec((m, d), lambda s: (0, 0)),
            pl.BlockSpec((m, hd), lambda s: (0, X_HEADS + s)),
            pl.BlockSpec((hd, d), lambda s: (s, 0)),
        ],
        out_specs=[pl.BlockSpec((d // X_HEADS, X_HEADS * m), lambda s: (s, 0)),
                   pl.BlockSpec((m, d), lambda s: (s, 0))],
        out_shape=[jax.ShapeDtypeStruct((d, X_HEADS * m), BF16),
                   jax.ShapeDtypeStruct((X_HEADS * m, d), BF16)],
        compiler_params=pltpu.CompilerParams(dimension_semantics=_ARB1, vmem_limit_bytes=limit),
        name="absorb",
    )(wq, kv, kv, wo)


def _cross_kernel(h_ref, g_ref, qk_ref, vo_ref, o_ref):
    x = h_ref[...]
    m = qk_ref.shape[1] // X_HEADS
    scale = (x.shape[1] // X_HEADS) ** -0.5
    hn = _rmsnorm(x, g_ref[...]).astype(BF16)
    probs = []
    for head in range(X_HEADS):
        sc = _dot(hn, qk_ref[:, head * m:(head + 1) * m]) * scale
        mx = jnp.max(sc, axis=1, keepdims=True)
        e = jnp.exp(sc - mx)
        probs.append((e * (1.0 / jnp.sum(e, axis=1, keepdims=True))).astype(BF16))
    o_ref[...] = x + _dot(jnp.concatenate(probs, axis=1), vo_ref[...])


def _cross(h, g, qk, vo, *, tm=1024):
    s, d = h.shape
    n = qk.shape[1]
    limit = _vmem_limit(
        4 * _nbytes((tm, d), F32), 2 * 2 * _nbytes((d, n), BF16),
        3 * _nbytes((tm, d), F32), _nbytes((tm, d), BF16), 6 * _nbytes((tm, n), F32))
    return pl.pallas_call(
        _cross_kernel,
        grid=(s // tm,),
        in_specs=[
            pl.BlockSpec((tm, d), lambda i: (i, 0)),
            pl.BlockSpec((1, d), lambda i: (0, 0)),
            pl.BlockSpec((d, n), lambda i: (0, 0)),
            pl.BlockSpec((n, d), lambda i: (0, 0)),
        ],
        out_specs=pl.BlockSpec((tm, d), lambda i: (i, 0)),
        out_shape=jax.ShapeDtypeStruct((s, d), F32),
        compiler_params=pltpu.CompilerParams(dimension_semantics=_ARB1, vmem_limit_bytes=limit),
        name="cross_attn",
    )(h, g, qk, vo)


def _row(v):
    return v.reshape(1, -1)


def kernel(x, mem, g_ffn1, w1_gate, w1_up, w1_down, g_mix, w_in, g_v, w_s, b_s, sinks, g_a_out,
           g_b_out, w_out, g_x, g_mem, w_xq, w_xkv, w_xo, g_ffn2, w2_gate, w2_up, w2_down, g_final):
    depth = g_ffn1.shape[0]
    d_a = g_v.shape[1]
    d_b = g_b_out.shape[1]
    d_kv = 2 * B_KV_HEADS * HEAD_DIM
    outs = []
    bf16_weights = {}
    for b in range(x.shape[0]):
        h = x[b]
        for l in range(depth):
            casts = () if l in bf16_weights else (w_in[l], w_out[l], w_xq[l], w_xo[l])
            h, *copies = _ffn(h, _row(g_ffn1[l]), w1_gate[l], w1_up[l], w1_down[l], _row(g_final),
                              casts, final_norm=False)
            w_in16, w_out16, w_xq16, w_xo16 = bf16_weights.setdefault(l, copies)

            u, v, q, kv = _in_proj(h, _row(g_mix[l]), w_in16, d_a, d_b, d_kv)
            h = _mixer(sinks[l], u, v, q, kv, h, _row(g_v[l]), w_s[l], jnp.transpose(b_s[l]),
                       _row(g_a_out[l]), _row(g_b_out[l]), w_out16)

            mem_kv = _mem_kv(mem[b], _row(g_mem[l]), w_xkv[l])
            qk, vo = _absorb(w_xq16, mem_kv, w_xo16)
            h = _cross(h, _row(g_x[l]), qk, vo)

            h, = _ffn(h, _row(g_ffn2[l]), w2_gate[l], w2_up[l], w2_down[l], _row(g_final),
                      final_norm=(l == depth - 1))
        outs.append(h)
    return outs[0][None] if len(outs) == 1 else jnp.stack(outs, axis=0)
```

```python
import functools

import jax
import jax.numpy as jnp
from jax import lax
from jax.experimental import pallas as pl
from jax.experimental.pallas import tpu as pltpu

F32 = jnp.float32
BF16 = jnp.bfloat16

EPS = 1e-5
NEG = -1e30
CHUNK = 128
A_GROUPS = 8
HEAD_DIM = 64
B_KV_HEADS = 2
X_HEADS = 4

V7X_VMEM_BYTES = 64 * 2**20
V7X_LANES = 128
V7X_MXU_N = 256
CAST_ROWS = 32
PROJ_M = 256
PROJ_N = 256


def _vmem_limit(*byte_counts):
    return int(min(sum(byte_counts), V7X_VMEM_BYTES))


def _nbytes(shape, dtype):
    n = 1
    for s in shape:
        n *= s
    return n * jnp.dtype(dtype).itemsize


def _rmsnorm(x, g):
    ms = jnp.mean(x * x, axis=-1, keepdims=True)
    return x * lax.rsqrt(ms + EPS) * g


def _gelu(x):
    return 0.5 * x * (1.0 + lax.erf(x * (2.0 ** -0.5)))


def _dot(a, b):
    return jnp.dot(a, b, preferred_element_type=F32)


def _dot_nt(a, b):
    return lax.dot_general(a, b, (((1,), (1,)), ((), ())), preferred_element_type=F32)


def _load_resident(pairs):
    copies = [pltpu.make_async_copy(src, dst, sem) for src, dst, sem in pairs]

    @pl.when(pl.program_id(0) == 0)
    def _():
        for cp in copies:
            cp.start()
        for cp in copies:
            cp.wait()


_HBM = pl.BlockSpec(memory_space=pl.ANY)
_ARB1 = ("arbitrary",)
_ARB2 = ("arbitrary", "arbitrary")


def _ffn_kernel(h_hbm, g_ref, wg_ref, wu_ref, wd_ref, gf_ref, *rest, down_n, final_norm, n_casts):
    cast_in, o_ref, cast_out = rest[:n_casts], rest[n_casts], rest[n_casts + 1:2 * n_casts + 1]
    hn_ref, hbuf_ref, h_sem = rest[2 * n_casts + 1:]
    i, j = pl.program_id(0), pl.program_id(1)
    tm = hbuf_ref.shape[0]

    def h_copy(tile):
        rows = pl.ds(pl.multiple_of(tile * tm, tm), tm)
        return pltpu.make_async_copy(h_hbm.at[rows, :], hbuf_ref, h_sem)

    for src, dst in zip(cast_in, cast_out):
        dst[...] = src[...].astype(BF16)

    @pl.when(jnp.logical_and(i == 0, j == 0))
    def _():
        h_copy(0).start()

    @pl.when(j == 0)
    def _():
        h_copy(i).wait()
        x = hbuf_ref[...]
        hn_ref[...] = _rmsnorm(x, g_ref[...]).astype(BF16)
        o_ref[...] = x

    @pl.when(jnp.logical_and(j == 1, i + 1 < pl.num_programs(0)))
    def _():
        h_copy(i + 1).start()

    hn = hn_ref[...]
    tf = wg_ref.shape[1]
    act = []
    for c in range(0, tf, V7X_MXU_N):
        cols = slice(c, c + V7X_MXU_N)
        gate = _dot(hn, wg_ref[:, cols].astype(BF16))
        up = _dot(hn, wu_ref[:, cols].astype(BF16))
        act.append(((0.5 * jax.nn.silu(gate)) * up).astype(BF16))
    act = act[0] if len(act) == 1 else jnp.concatenate(act, axis=1)
    for n in range(0, o_ref.shape[1], down_n):
        cols = slice(n, n + down_n)
        o_ref[:, cols] += _dot(act, wd_ref[:, cols].astype(BF16))

    if final_norm:
        @pl.when(j == pl.num_programs(1) - 1)
        def _():
            o_ref[...] = _rmsnorm(o_ref[...], gf_ref[...])


def _ffn(h, g, wg, wu, wd, g_final, casts=(), *, final_norm, tm=1024, tf=512, down_n=512):
    s, d = h.shape
    f = wg.shape[1]
    n_i, n_j = s // tm, f // tf
    band = lambda i, j: (jnp.minimum(i * n_j + j, -(-d // CAST_ROWS) - 1), 0)
    assert all(w.shape[0] == d and d % CAST_ROWS == 0 and d // CAST_ROWS <= n_i * n_j for w in casts)
    cast_specs = lambda: [pl.BlockSpec((CAST_ROWS, w.shape[1]), band) for w in casts]
    limit = _vmem_limit(
        _nbytes((tm, d), F32), 2 * _nbytes((tm, d), F32), _nbytes((tm, d), BF16),
        2 * 3 * _nbytes((d, tf), F32), 3 * _nbytes((d, tf), BF16),
        4 * _nbytes((tm, V7X_MXU_N), F32), _nbytes((tm, tf), BF16),
        2 * _nbytes((tm, down_n), F32), _nbytes((tm, d), F32),
        *[3 * _nbytes((CAST_ROWS, w.shape[1]), F32) for w in casts])
    outs = pl.pallas_call(
        functools.partial(_ffn_kernel, down_n=down_n, final_norm=final_norm, n_casts=len(casts)),
        grid=(n_i, n_j),
        in_specs=[
            pl.BlockSpec(memory_space=pl.ANY),
            pl.BlockSpec((1, d), lambda i, j: (0, 0)),
            pl.BlockSpec((d, tf), lambda i, j: (0, j)),
            pl.BlockSpec((d, tf), lambda i, j: (0, j)),
            pl.BlockSpec((tf, d), lambda i, j: (j, 0)),
            pl.BlockSpec((1, d), lambda i, j: (0, 0)),
            *cast_specs(),
        ],
        out_specs=[pl.BlockSpec((tm, d), lambda i, j: (i, 0)), *cast_specs()],
        out_shape=[jax.ShapeDtypeStruct((s, d), F32),
                   *[jax.ShapeDtypeStruct(w.shape, BF16) for w in casts]],
        scratch_shapes=[pltpu.VMEM((tm, d), BF16), pltpu.VMEM((tm, d), F32),
                        pltpu.SemaphoreType.DMA(())],
        compiler_params=pltpu.CompilerParams(dimension_semantics=_ARB2, vmem_limit_bytes=limit),
        name="ffn",
    )(h, g, wg, wu, wd, g_final, *casts)
    return outs


def _in_proj_kernel(h_ref, g_ref, w_hbm, u_ref, v_ref, q_ref, kv_ref, w_ref, w_sem):
    _load_resident([(w_hbm, w_ref, w_sem)])
    hn = _rmsnorm(h_ref[...], g_ref[...]).astype(BF16)
    d_a = u_ref.shape[1]
    d_b = q_ref.shape[1]

    def proj(lo):
        return _dot(hn, w_ref[:, lo:lo + V7X_MXU_N])

    for c in range(0, d_a, V7X_MXU_N):
        u_ref[:, c:c + V7X_MXU_N] = _gelu(proj(c)).astype(BF16)
    for c in range(0, d_a, V7X_MXU_N):
        v_ref[:, c:c + V7X_MXU_N] = _gelu(proj(d_a + c)).astype(BF16)
    for c in range(0, d_b, V7X_MXU_N):
        q_ref[:, c:c + V7X_MXU_N] = proj(2 * d_a + c).astype(BF16)
    for c in range(0, kv_ref.shape[1], V7X_MXU_N):
        kv_ref[:, c:c + V7X_MXU_N] = proj(2 * d_a + d_b + c).astype(BF16)


def _in_proj(h, g, w, d_a, d_b, d_kv, *, tm=1024):
    s, d = h.shape
    n = w.shape[1]
    limit = _vmem_limit(
        2 * _nbytes((tm, d), F32), _nbytes((d, n), BF16), 2 * _nbytes((tm, n), BF16),
        _nbytes((tm, d), F32), _nbytes((tm, d), BF16), 6 * _nbytes((tm, V7X_MXU_N), F32))
    return pl.pallas_call(
        _in_proj_kernel,
        grid=(s // tm,),
        in_specs=[
            pl.BlockSpec((tm, d), lambda i: (i, 0)),
            pl.BlockSpec((1, d), lambda i: (0, 0)),
            _HBM,
        ],
        out_specs=[
            pl.BlockSpec((tm, d_a), lambda i: (i, 0)),
            pl.BlockSpec((tm, d_a), lambda i: (i, 0)),
            pl.BlockSpec((tm, d_b), lambda i: (i, 0)),
            pl.BlockSpec((tm, d_kv), lambda i: (i, 0)),
        ],
        out_shape=[
            jax.ShapeDtypeStruct((s, d_a), BF16),
            jax.ShapeDtypeStruct((s, d_a), BF16),
            jax.ShapeDtypeStruct((s, d_b), BF16),
            jax.ShapeDtypeStruct((s, d_kv), BF16),
        ],
        scratch_shapes=[pltpu.VMEM((d, n), BF16), pltpu.SemaphoreType.DMA(())],
        compiler_params=pltpu.CompilerParams(dimension_semantics=_ARB1, vmem_limit_bytes=limit),
        name="in_proj",
    )(h, g, w)


def _mixer_kernel(sinks_ref, u_ref, v_ref, q_ref, kvc_ref, kvp_ref, h_ref, gv_ref, ws_ref, bst_ref,
                  ga_ref, gb_ref, wout_hbm, o_ref, kvbuf_ref, wsm_ref, y_ref, wout_ref, wout_sem):
    step = pl.program_id(0)
    _load_resident([(wout_hbm, wout_ref, wout_sem)])

    @pl.when(step == 0)
    def _():
        y_ref[1] = jnp.zeros(y_ref.shape[1:], BF16)
        t_idx = lax.broadcasted_iota(jnp.int32, (CHUNK, CHUNK), 0)
        s_idx = lax.broadcasted_iota(jnp.int32, (CHUNK, CHUNK), 1)
        for g in range(A_GROUPS):
            wsm_ref[g] = jnp.where(s_idx <= t_idx, ws_ref[g], 0.0).astype(BF16)

    refs = (sinks_ref, u_ref, v_ref, q_ref, kvc_ref, kvp_ref, h_ref, gv_ref, bst_ref, ga_ref,
            gb_ref, o_ref, kvbuf_ref, wsm_ref, wout_ref)
    for parity in range(2):
        @pl.when(step % 2 == parity)
        def _():
            _mix_and_project(*refs, y_new=y_ref.at[parity], y_old=y_ref.at[1 - parity])


def _mix_and_project(sinks_ref, u_ref, v_ref, q_ref, kvc_ref, kvp_ref, h_ref, gv_ref, bst_ref, ga_ref,
                     gb_ref, o_ref, kvbuf_ref, wsm_ref, wout_ref, *, y_new, y_old):
    tm, d_a = u_ref.shape
    d_b = q_ref.shape[1]
    n_blocks = tm // CHUNK
    group = d_b // HEAD_DIM // B_KV_HEADS
    kv_w = B_KV_HEADS * HEAD_DIM
    tile = jnp.minimum(pl.program_id(0), pl.num_programs(0) - 2)

    kvbuf_ref[0:CHUNK, :] = kvp_ref[...]
    kvbuf_ref[CHUNK:, :] = kvc_ref[...]

    low_half = lax.broadcasted_iota(jnp.int32, (CHUNK, V7X_LANES), 1) < HEAD_DIM
    low2 = lax.broadcasted_iota(jnp.int32, (2 * CHUNK, V7X_LANES), 1) < HEAD_DIM
    top_half = lax.broadcasted_iota(jnp.int32, (V7X_LANES, CHUNK), 0) < HEAD_DIM
    k_pos = lax.broadcasted_iota(jnp.int32, (2 * CHUNK, CHUNK), 0)
    q_pos = lax.broadcasted_iota(jnp.int32, (2 * CHUNK, CHUNK), 1)
    band = (k_pos > q_pos) & (k_pos <= q_pos + CHUNK)

    pieces = [(r, c) for c in range(0, o_ref.shape[1], PROJ_N) for r in range(0, tm, PROJ_M)]
    n_pieces, n_stages = len(pieces), n_blocks * (1 + B_KV_HEADS * (group // 2))
    stage = [0]

    def project():
        stage[0] += 1
        while n_pieces - len(pieces) < stage[0] * n_pieces // n_stages:
            r, c = pieces.pop(0)
            rows, cols = slice(r, r + PROJ_M), slice(c, c + PROJ_N)
            o_ref[rows, cols] = h_ref[rows, cols] + _dot(y_old[rows, :], wout_ref[:, cols])

    for b in range(n_blocks):
        r0 = b * CHUNK
        blk = slice(r0, r0 + CHUNK)

        project()
        vn = _rmsnorm(v_ref[blk, :].astype(F32), gv_ref[...]).astype(BF16)
        ub = u_ref[blk, :].astype(F32)
        ya = []
        for g in range(A_GROUPS):
            cols = slice(g * CHUNK, (g + 1) * CHUNK)
            sv = _dot(wsm_ref[g], vn[:, cols]) + bst_ref[:, g:g + 1]
            ya.append(ub[:, cols] * sv)
        ya = jnp.concatenate(ya, axis=1)
        y_new[blk, 0:d_a] = _rmsnorm(ya, ga_ref[...]).astype(BF16)

        qb = q_ref[blk, :] * (HEAD_DIM ** -0.5)
        kk = kvbuf_ref[r0:r0 + 2 * CHUNK, :]
        k_f = kk[:, 0:kv_w].astype(F32)
        v_f = kk[:, kv_w:2 * kv_w].astype(F32)
        k_rot = pltpu.roll(k_f, HEAD_DIM, axis=1)
        v_rot = pltpu.roll(v_f, HEAD_DIM, axis=1)
        mask = band & (k_pos >= jnp.where(tile == 0, CHUNK, 0)) if b == 0 else band
        ybt = []
        for j in range(B_KV_HEADS):
            if j == 0:
                k2 = jnp.where(low2, k_f, k_rot).astype(BF16)
                v2t = jnp.where(low2, v_f, v_rot).T.astype(BF16)
            else:
                k2 = jnp.where(low2, k_rot, k_f).astype(BF16)
                v2t = jnp.where(low2, v_rot, v_f).T.astype(BF16)
            qm = []
            for hh in range(group):
                head = j * group + hh
                c0 = (head // 2) * V7X_LANES
                qc = qb[:, c0:c0 + V7X_LANES]
                keep = low_half if head % 2 == 0 else jnp.logical_not(low_half)
                qm.append(jnp.where(keep, qc, jnp.zeros_like(qc)))
            st = _dot_nt(k2, jnp.concatenate(qm, axis=0))
            pt = []
            for hh in range(group):
                sink = sinks_ref[j * group + hh]
                sc = jnp.where(mask, st[:, hh * CHUNK:(hh + 1) * CHUNK], NEG)
                m = jnp.maximum(jnp.max(sc, axis=0, keepdims=True), sink)
                e = jnp.exp(sc - m)
                denom = jnp.sum(e, axis=0, keepdims=True) + jnp.exp(sink - m)
                pt.append((e * (1.0 / denom)).astype(BF16))
                if hh % 2 == 1:
                    project()
            ot = _dot(v2t, jnp.concatenate(pt, axis=1))
            for pair in range(group // 2):
                even = ot[:, (2 * pair) * CHUNK:(2 * pair + 1) * CHUNK]
                odd = ot[:, (2 * pair + 1) * CHUNK:(2 * pair + 2) * CHUNK]
                ybt.append(jnp.where(top_half, even, odd))
        ybt = jnp.concatenate(ybt, axis=0)
        ybt = ybt * lax.rsqrt(jnp.mean(ybt * ybt, axis=0, keepdims=True) + EPS)
        y_new[blk, d_a:d_a + d_b] = (ybt.T * gb_ref[...]).astype(BF16)
    assert not pieces


def _mixer(sinks, u, v, q, kv, h, g_v, w_s, b_s_t, g_a, g_b, w_out, *, tm=512):
    s, d = h.shape
    d_a = u.shape[1]
    d_b = q.shape[1]
    d_kv = kv.shape[1]
    d_mix = d_a + d_b
    n_blocks = tm // CHUNK
    n_tiles = s // tm
    limit = _vmem_limit(
        2 * 2 * _nbytes((tm, d_a), BF16), 2 * _nbytes((tm, d_b), BF16),
        2 * _nbytes((tm + CHUNK, d_kv), BF16), 4 * _nbytes((tm, d), F32),
        _nbytes((d_mix, d), BF16), 2 * _nbytes((tm, d_mix), BF16),
        _nbytes((tm + CHUNK, d_kv), BF16), 2 * _nbytes((A_GROUPS, CHUNK, CHUNK), F32),
        2 * _nbytes((tm, d // n_blocks), F32), 32 * _nbytes((CHUNK, d_a), F32))
    vec = lambda n: pl.BlockSpec((1, n), lambda i: (0, 0))
    mixed = lambda i: (jnp.minimum(i, n_tiles - 1), 0)
    projected = lambda i: (jnp.maximum(i - 1, 0), 0)
    prev_block = lambda i: (jnp.maximum(jnp.minimum(i, n_tiles - 1) * n_blocks - 1, 0), 0)
    return pl.pallas_call(
        _mixer_kernel,
        grid=(n_tiles + 1,),
        in_specs=[
            pl.BlockSpec(memory_space=pltpu.SMEM),
            pl.BlockSpec((tm, d_a), mixed),
            pl.BlockSpec((tm, d_a), mixed),
            pl.BlockSpec((tm, d_b), mixed),
            pl.BlockSpec((tm, d_kv), mixed),
            pl.BlockSpec((CHUNK, d_kv), prev_block),
            pl.BlockSpec((tm, d), projected),
            vec(d_a),
            pl.BlockSpec((A_GROUPS, CHUNK, CHUNK), lambda i: (0, 0, 0)),
            pl.BlockSpec((CHUNK, A_GROUPS), lambda i: (0, 0)),
            vec(d_a),
            vec(d_b),
            _HBM,
        ],
        out_specs=pl.BlockSpec((tm, d), projected),
        out_shape=jax.ShapeDtypeStruct((s, d), F32),
        scratch_shapes=[
            pltpu.VMEM((tm + CHUNK, d_kv), BF16),
            pltpu.VMEM((A_GROUPS, CHUNK, CHUNK), BF16),
            pltpu.VMEM((2, tm, d_mix), BF16),
            pltpu.VMEM((d_mix, d), BF16),
            pltpu.SemaphoreType.DMA(()),
        ],
        compiler_params=pltpu.CompilerParams(dimension_semantics=_ARB1, vmem_limit_bytes=limit),
        name="mixer",
    )(sinks, u, v, q, kv, kv, h, g_v, w_s, b_s_t, g_a, g_b, w_out)


def _mem_absorb_kernel(mem_ref, g_ref, wk_ref, wv_ref, wq_ref, wo_ref, qk_ref, vo_ref):
    mn = _rmsnorm(mem_ref[...], g_ref[...]).astype(BF16)
    k = _dot(mn, wk_ref[...].astype(BF16)).astype(BF16)
    v = _dot(mn, wv_ref[...].astype(BF16)).astype(BF16)
    qk_ref[...] = _dot_nt(wq_ref[...], k).astype(BF16)
    vo_ref[...] = _dot(v, wo_ref[...]).astype(BF16)


def _mem_absorb(mem, g, w_kv, wq, wo):
    m, d = mem.shape
    hd = d // X_HEADS
    limit = V7X_VMEM_BYTES
    return pl.pallas_call(
        _mem_absorb_kernel,
        grid=(X_HEADS,),
        in_specs=[
            pl.BlockSpec((m, d), lambda h: (0, 0)),
            pl.BlockSpec((1, d), lambda h: (0, 0)),
            pl.BlockSpec((d, hd), lambda h: (0, h)),
            pl.BlockSpec((d, hd), lambda h: (0, X_HEADS + h)),
            pl.BlockSpec((d, hd), lambda h: (0, h)),
            pl.BlockSpec((hd, d), lambda h: (h, 0)),
        ],
        out_specs=[pl.BlockSpec((d, m), lambda h: (0, h)), pl.BlockSpec((m, d), lambda h: (h, 0))],
        out_shape=[jax.ShapeDtypeStruct((d, X_HEADS * m), BF16),
                   jax.ShapeDtypeStruct((X_HEADS * m, d), BF16)],
        compiler_params=pltpu.CompilerParams(dimension_semantics=_ARB1, vmem_limit_bytes=limit),
        name="mem_absorb",
    )(mem, g, w_kv, w_kv, wq, wo)


def _cross_kernel(h_ref, g_ref, qk_ref, vo_ref, o_ref):
    x = h_ref[...]
    m = qk_ref.shape[1] // X_HEADS
    scale = (x.shape[1] // X_HEADS) ** -0.5
    hn = _rmsnorm(x, g_ref[...]).astype(BF16)
    probs = []
    for head in range(X_HEADS):
        sc = _dot(hn, qk_ref[:, head * m:(head + 1) * m]) * scale
        mx = jnp.max(sc, axis=1, keepdims=True)
        e = jnp.exp(sc - mx)
        probs.append((e * (1.0 / jnp.sum(e, axis=1, keepdims=True))).astype(BF16))
    o_ref[...] = x + _dot(jnp.concatenate(probs, axis=1), vo_ref[...])


def _cross(h, g, qk, vo, *, tm=1024):
    s, d = h.shape
    n = qk.shape[1]
    limit = _vmem_limit(
        4 * _nbytes((tm, d), F32), 2 * 2 * _nbytes((d, n), BF16),
        3 * _nbytes((tm, d), F32), _nbytes((tm, d), BF16), 6 * _nbytes((tm, n), F32))
    return pl.pallas_call(
        _cross_kernel,
        grid=(s // tm,),
        in_specs=[
            pl.BlockSpec((tm, d), lambda i: (i, 0)),
            pl.BlockSpec((1, d), lambda i: (0, 0)),
            pl.BlockSpec((d, n), lambda i: (0, 0)),
            pl.BlockSpec((n, d), lambda i: (0, 0)),
        ],
        out_specs=pl.BlockSpec((tm, d), lambda i: (i, 0)),
        out_shape=jax.ShapeDtypeStruct((s, d), F32),
        compiler_params=pltpu.CompilerParams(dimension_semantics=_ARB1, vmem_limit_bytes=limit),
        name="cross_attn",
    )(h, g, qk, vo)


def _row(v):
    return v.reshape(1, -1)


def kernel(x, mem, g_ffn1, w1_gate, w1_up, w1_down, g_mix, w_in, g_v, w_s, b_s, sinks, g_a_out,
           g_b_out, w_out, g_x, g_mem, w_xq, w_xkv, w_xo, g_ffn2, w2_gate, w2_up, w2_down, g_final):
    depth = g_ffn1.shape[0]
    d_a = g_v.shape[1]
    d_b = g_b_out.shape[1]
    d_kv = 2 * B_KV_HEADS * HEAD_DIM
    outs = []
    bf16_weights = {}
    for b in range(x.shape[0]):
        h = x[b]
        for l in range(depth):
            casts = () if l in bf16_weights else (w_in[l], w_out[l], w_xq[l], w_xo[l])
            h, *copies = _ffn(h, _row(g_ffn1[l]), w1_gate[l], w1_up[l], w1_down[l], _row(g_final),
                              casts, final_norm=False)
            w_in16, w_out16, w_xq16, w_xo16 = bf16_weights.setdefault(l, copies)

            u, v, q, kv = _in_proj(h, _row(g_mix[l]), w_in16, d_a, d_b, d_kv)
            h = _mixer(sinks[l], u, v, q, kv, h, _row(g_v[l]), w_s[l], jnp.transpose(b_s[l]),
                       _row(g_a_out[l]), _row(g_b_out[l]), w_out16)

            qk, vo = _mem_absorb(mem[b], _row(g_mem[l]), w_xkv[l], w_xq16, w_xo16)
            h = _cross(h, _row(g_x[l]), qk, vo)

            h, = _ffn(h, _row(g_ffn2[l]), w2_gate[l], w2_up[l], w2_down[l], _row(g_final),
                      final_norm=(l == depth - 1))
        outs.append(h)
    return outs[0][None] if len(outs) == 1 else jnp.stack(outs, axis=0)
```

```python
import functools

import jax
import jax.numpy as jnp
from jax import lax
from jax.experimental import pallas as pl
from jax.experimental.pallas import tpu as pltpu

F32 = jnp.float32
BF16 = jnp.bfloat16

EPS = 1e-5
NEG = -1e30
CHUNK = 128
A_GROUPS = 8
HEAD_DIM = 64
B_KV_HEADS = 2
X_HEADS = 4

V7X_VMEM_BYTES = 64 * 2**20
V7X_LANES = 128
V7X_MXU_N = 256
CAST_ROWS = 32
PROJ_M = 256
PROJ_N = 256


def _vmem_limit(*byte_counts):
    return int(min(sum(byte_counts), V7X_VMEM_BYTES))


def _nbytes(shape, dtype):
    n = 1
    for s in shape:
        n *= s
    return n * jnp.dtype(dtype).itemsize


def _rmsnorm(x, g):
    ms = jnp.mean(x * x, axis=-1, keepdims=True)
    return x * lax.rsqrt(ms + EPS) * g


def _gelu(x):
    return 0.5 * x * (1.0 + lax.erf(x * (2.0 ** -0.5)))


def _dot(a, b):
    return jnp.dot(a, b, preferred_element_type=F32)


def _dot_nt(a, b):
    return lax.dot_general(a, b, (((1,), (1,)), ((), ())), preferred_element_type=F32)


def _resident(shape):
    return pl.BlockSpec(shape, lambda *_: (0,) * len(shape), pipeline_mode=pl.Buffered(1))


_ARB1 = ("arbitrary",)
_ARB2 = ("arbitrary", "arbitrary")


def _ffn_kernel(h_hbm, g_ref, wg_ref, wu_ref, wd_ref, gf_ref, *rest, down_n, final_norm, n_casts):
    cast_in, o_ref, cast_out = rest[:n_casts], rest[n_casts], rest[n_casts + 1:2 * n_casts + 1]
    hn_ref, hbuf_ref, h_sem = rest[2 * n_casts + 1:]
    i, j = pl.program_id(0), pl.program_id(1)
    tm = hbuf_ref.shape[0]

    def h_copy(tile):
        rows = pl.ds(pl.multiple_of(tile * tm, tm), tm)
        return pltpu.make_async_copy(h_hbm.at[rows, :], hbuf_ref, h_sem)

    for src, dst in zip(cast_in, cast_out):
        dst[...] = src[...].astype(BF16)

    @pl.when(jnp.logical_and(i == 0, j == 0))
    def _():
        h_copy(0).start()

    @pl.when(j == 0)
    def _():
        h_copy(i).wait()
        x = hbuf_ref[...]
        hn_ref[...] = _rmsnorm(x, g_ref[...]).astype(BF16)
        o_ref[...] = x

    @pl.when(jnp.logical_and(j == 1, i + 1 < pl.num_programs(0)))
    def _():
        h_copy(i + 1).start()

    hn = hn_ref[...]
    tf = wg_ref.shape[1]
    act = []
    for c in range(0, tf, V7X_MXU_N):
        cols = slice(c, c + V7X_MXU_N)
        gate = _dot(hn, wg_ref[:, cols].astype(BF16))
        up = _dot(hn, wu_ref[:, cols].astype(BF16))
        act.append(((0.5 * jax.nn.silu(gate)) * up).astype(BF16))
    act = act[0] if len(act) == 1 else jnp.concatenate(act, axis=1)
    for n in range(0, o_ref.shape[1], down_n):
        cols = slice(n, n + down_n)
        o_ref[:, cols] += _dot(act, wd_ref[:, cols].astype(BF16))

    if final_norm:
        @pl.when(j == pl.num_programs(1) - 1)
        def _():
            o_ref[...] = _rmsnorm(o_ref[...], gf_ref[...])


def _ffn(h, g, wg, wu, wd, g_final, casts=(), *, final_norm, tm=1024, tf=512, down_n=512):
    s, d = h.shape
    f = wg.shape[1]
    n_i, n_j = s // tm, f // tf
    band = lambda i, j: (jnp.minimum(i * n_j + j, -(-d // CAST_ROWS) - 1), 0)
    assert all(w.shape[0] == d and d % CAST_ROWS == 0 and d // CAST_ROWS <= n_i * n_j for w in casts)
    cast_specs = lambda: [pl.BlockSpec((CAST_ROWS, w.shape[1]), band) for w in casts]
    limit = _vmem_limit(
        _nbytes((tm, d), F32), 2 * _nbytes((tm, d), F32), _nbytes((tm, d), BF16),
        2 * 3 * _nbytes((d, tf), F32), 3 * _nbytes((d, tf), BF16),
        4 * _nbytes((tm, V7X_MXU_N), F32), _nbytes((tm, tf), BF16),
        2 * _nbytes((tm, down_n), F32), _nbytes((tm, d), F32),
        *[3 * _nbytes((CAST_ROWS, w.shape[1]), F32) for w in casts])
    outs = pl.pallas_call(
        functools.partial(_ffn_kernel, down_n=down_n, final_norm=final_norm, n_casts=len(casts)),
        grid=(n_i, n_j),
        in_specs=[
            pl.BlockSpec(memory_space=pl.ANY),
            pl.BlockSpec((1, d), lambda i, j: (0, 0)),
            pl.BlockSpec((d, tf), lambda i, j: (0, j)),
            pl.BlockSpec((d, tf), lambda i, j: (0, j)),
            pl.BlockSpec((tf, d), lambda i, j: (j, 0)),
            pl.BlockSpec((1, d), lambda i, j: (0, 0)),
            *cast_specs(),
        ],
        out_specs=[pl.BlockSpec((tm, d), lambda i, j: (i, 0)), *cast_specs()],
        out_shape=[jax.ShapeDtypeStruct((s, d), F32),
                   *[jax.ShapeDtypeStruct(w.shape, BF16) for w in casts]],
        scratch_shapes=[pltpu.VMEM((tm, d), BF16), pltpu.VMEM((tm, d), F32),
                        pltpu.SemaphoreType.DMA(())],
        compiler_params=pltpu.CompilerParams(dimension_semantics=_ARB2, vmem_limit_bytes=limit),
        name="ffn",
    )(h, g, wg, wu, wd, g_final, *casts)
    return outs


def _in_proj_kernel(h_ref, g_ref, w_ref, u_ref, v_ref, q_ref, kv_ref):
    hn = _rmsnorm(h_ref[...], g_ref[...]).astype(BF16)
    d_a = u_ref.shape[1]
    d_b = q_ref.shape[1]

    def proj(lo):
        return _dot(hn, w_ref[:, lo:lo + V7X_MXU_N])

    for c in range(0, d_a, V7X_MXU_N):
        u_ref[:, c:c + V7X_MXU_N] = _gelu(proj(c)).astype(BF16)
    for c in range(0, d_a, V7X_MXU_N):
        v_ref[:, c:c + V7X_MXU_N] = _gelu(proj(d_a + c)).astype(BF16)
    for c in range(0, d_b, V7X_MXU_N):
        q_ref[:, c:c + V7X_MXU_N] = proj(2 * d_a + c).astype(BF16)
    for c in range(0, kv_ref.shape[1], V7X_MXU_N):
        kv_ref[:, c:c + V7X_MXU_N] = proj(2 * d_a + d_b + c).astype(BF16)


def _in_proj(h, g, w, d_a, d_b, d_kv, *, tm=1024):
    s, d = h.shape
    n = w.shape[1]
    limit = _vmem_limit(
        2 * _nbytes((tm, d), F32), _nbytes((d, n), BF16), 2 * _nbytes((tm, n), BF16),
        _nbytes((tm, d), F32), _nbytes((tm, d), BF16), 6 * _nbytes((tm, V7X_MXU_N), F32))
    return pl.pallas_call(
        _in_proj_kernel,
        grid=(s // tm,),
        in_specs=[
            pl.BlockSpec((tm, d), lambda i: (i, 0)),
            pl.BlockSpec((1, d), lambda i: (0, 0)),
            _resident((d, n)),
        ],
        out_specs=[
            pl.BlockSpec((tm, d_a), lambda i: (i, 0)),
            pl.BlockSpec((tm, d_a), lambda i: (i, 0)),
            pl.BlockSpec((tm, d_b), lambda i: (i, 0)),
            pl.BlockSpec((tm, d_kv), lambda i: (i, 0)),
        ],
        out_shape=[
            jax.ShapeDtypeStruct((s, d_a), BF16),
            jax.ShapeDtypeStruct((s, d_a), BF16),
            jax.ShapeDtypeStruct((s, d_b), BF16),
            jax.ShapeDtypeStruct((s, d_kv), BF16),
        ],
        compiler_params=pltpu.CompilerParams(dimension_semantics=_ARB1, vmem_limit_bytes=limit),
        name="in_proj",
    )(h, g, w)


def _mixer_kernel(sinks_ref, u_ref, v_ref, q_ref, kvc_ref, kvp_ref, h_ref, gv_ref, ws_ref, bst_ref,
                  ga_ref, gb_ref, wout_ref, o_ref, kvbuf_ref, wsm_ref, y_ref):
    tm, d_a = u_ref.shape
    d_b = q_ref.shape[1]
    n_blocks = tm // CHUNK
    group = d_b // HEAD_DIM // B_KV_HEADS
    kv_w = B_KV_HEADS * HEAD_DIM
    step = pl.program_id(0)
    tile = jnp.minimum(step, pl.num_programs(0) - 2)
    y_new = y_ref.at[step % 2]
    y_old = y_ref.at[(step + 1) % 2]

    @pl.when(step == 0)
    def _():
        y_ref[1] = jnp.zeros(y_ref.shape[1:], BF16)
        t_idx = lax.broadcasted_iota(jnp.int32, (CHUNK, CHUNK), 0)
        s_idx = lax.broadcasted_iota(jnp.int32, (CHUNK, CHUNK), 1)
        for g in range(A_GROUPS):
            wsm_ref[g] = jnp.where(s_idx <= t_idx, ws_ref[g], 0.0).astype(BF16)

    kvbuf_ref[0:CHUNK, :] = kvp_ref[...]
    kvbuf_ref[CHUNK:, :] = kvc_ref[...]

    low_half = lax.broadcasted_iota(jnp.int32, (CHUNK, V7X_LANES), 1) < HEAD_DIM
    low2 = lax.broadcasted_iota(jnp.int32, (2 * CHUNK, V7X_LANES), 1) < HEAD_DIM
    top_half = lax.broadcasted_iota(jnp.int32, (V7X_LANES, CHUNK), 0) < HEAD_DIM
    k_pos = lax.broadcasted_iota(jnp.int32, (2 * CHUNK, CHUNK), 0)
    q_pos = lax.broadcasted_iota(jnp.int32, (2 * CHUNK, CHUNK), 1)
    band = (k_pos > q_pos) & (k_pos <= q_pos + CHUNK)

    pieces = [(r, c) for c in range(0, o_ref.shape[1], PROJ_N) for r in range(0, tm, PROJ_M)]
    n_pieces, n_stages = len(pieces), n_blocks * (1 + B_KV_HEADS * (group // 2))
    stage = [0]

    def project():
        stage[0] += 1
        while n_pieces - len(pieces) < stage[0] * n_pieces // n_stages:
            r, c = pieces.pop(0)
            rows, cols = slice(r, r + PROJ_M), slice(c, c + PROJ_N)
            o_ref[rows, cols] = h_ref[rows, cols] + _dot(y_old[rows, :], wout_ref[:, cols])

    for b in range(n_blocks):
        r0 = b * CHUNK
        blk = slice(r0, r0 + CHUNK)

        project()
        vn = _rmsnorm(v_ref[blk, :].astype(F32), gv_ref[...]).astype(BF16)
        ub = u_ref[blk, :].astype(F32)
        ya = []
        for g in range(A_GROUPS):
            cols = slice(g * CHUNK, (g + 1) * CHUNK)
            sv = _dot(wsm_ref[g], vn[:, cols]) + bst_ref[:, g:g + 1]
            ya.append(ub[:, cols] * sv)
        ya = jnp.concatenate(ya, axis=1)
        y_new[blk, 0:d_a] = _rmsnorm(ya, ga_ref[...]).astype(BF16)

        qb = q_ref[blk, :] * (HEAD_DIM ** -0.5)
        kk = kvbuf_ref[r0:r0 + 2 * CHUNK, :]
        k_f = kk[:, 0:kv_w].astype(F32)
        v_f = kk[:, kv_w:2 * kv_w].astype(F32)
        k_rot = pltpu.roll(k_f, HEAD_DIM, axis=1)
        v_rot = pltpu.roll(v_f, HEAD_DIM, axis=1)
        mask = band & (k_pos >= jnp.where(tile == 0, CHUNK, 0)) if b == 0 else band
        ybt = []
        for j in range(B_KV_HEADS):
            if j == 0:
                k2 = jnp.where(low2, k_f, k_rot).astype(BF16)
                v2t = jnp.where(low2, v_f, v_rot).T.astype(BF16)
            else:
                k2 = jnp.where(low2, k_rot, k_f).astype(BF16)
                v2t = jnp.where(low2, v_rot, v_f).T.astype(BF16)
            qm = []
            for hh in range(group):
                head = j * group + hh
                c0 = (head // 2) * V7X_LANES
                qc = qb[:, c0:c0 + V7X_LANES]
                keep = low_half if head % 2 == 0 else jnp.logical_not(low_half)
                qm.append(jnp.where(keep, qc, jnp.zeros_like(qc)))
            st = _dot_nt(k2, jnp.concatenate(qm, axis=0))
            pt = []
            for hh in range(group):
                sink = sinks_ref[j * group + hh]
                sc = jnp.where(mask, st[:, hh * CHUNK:(hh + 1) * CHUNK], NEG)
                m = jnp.maximum(jnp.max(sc, axis=0, keepdims=True), sink)
                e = jnp.exp(sc - m)
                denom = jnp.sum(e, axis=0, keepdims=True) + jnp.exp(sink - m)
                pt.append((e * (1.0 / denom)).astype(BF16))
                if hh % 2 == 1:
                    project()
            ot = _dot(v2t, jnp.concatenate(pt, axis=1))
            for pair in range(group // 2):
                even = ot[:, (2 * pair) * CHUNK:(2 * pair + 1) * CHUNK]
                odd = ot[:, (2 * pair + 1) * CHUNK:(2 * pair + 2) * CHUNK]
                ybt.append(jnp.where(top_half, even, odd))
        ybt = jnp.concatenate(ybt, axis=0)
        ybt = ybt * lax.rsqrt(jnp.mean(ybt * ybt, axis=0, keepdims=True) + EPS)
        y_new[blk, d_a:d_a + d_b] = (ybt.T * gb_ref[...]).astype(BF16)
    assert not pieces


def _mixer(sinks, u, v, q, kv, h, g_v, w_s, b_s_t, g_a, g_b, w_out, *, tm=512):
    s, d = h.shape
    d_a = u.shape[1]
    d_b = q.shape[1]
    d_kv = kv.shape[1]
    d_mix = d_a + d_b
    n_blocks = tm // CHUNK
    n_tiles = s // tm
    limit = _vmem_limit(
        2 * 2 * _nbytes((tm, d_a), BF16), 2 * _nbytes((tm, d_b), BF16),
        2 * _nbytes((tm + CHUNK, d_kv), BF16), 4 * _nbytes((tm, d), F32),
        _nbytes((d_mix, d), BF16), 2 * _nbytes((tm, d_mix), BF16),
        _nbytes((tm + CHUNK, d_kv), BF16), 2 * _nbytes((A_GROUPS, CHUNK, CHUNK), F32),
        2 * _nbytes((tm, d // n_blocks), F32), 32 * _nbytes((CHUNK, d_a), F32))
    vec = lambda n: pl.BlockSpec((1, n), lambda i: (0, 0))
    mixed = lambda i: (jnp.minimum(i, n_tiles - 1), 0)
    projected = lambda i: (jnp.maximum(i - 1, 0), 0)
    prev_block = lambda i: (jnp.maximum(jnp.minimum(i, n_tiles - 1) * n_blocks - 1, 0), 0)
    return pl.pallas_call(
        _mixer_kernel,
        grid=(n_tiles + 1,),
        in_specs=[
            pl.BlockSpec(memory_space=pltpu.SMEM),
            pl.BlockSpec((tm, d_a), mixed),
            pl.BlockSpec((tm, d_a), mixed),
            pl.BlockSpec((tm, d_b), mixed),
            pl.BlockSpec((tm, d_kv), mixed),
            pl.BlockSpec((CHUNK, d_kv), prev_block),
            pl.BlockSpec((tm, d), projected),
            vec(d_a),
            pl.BlockSpec((A_GROUPS, CHUNK, CHUNK), lambda i: (0, 0, 0)),
            pl.BlockSpec((CHUNK, A_GROUPS), lambda i: (0, 0)),
            vec(d_a),
            vec(d_b),
            _resident((d_mix, d)),
        ],
        out_specs=pl.BlockSpec((tm, d), projected),
        out_shape=jax.ShapeDtypeStruct((s, d), F32),
        scratch_shapes=[
            pltpu.VMEM((tm + CHUNK, d_kv), BF16),
            pltpu.VMEM((A_GROUPS, CHUNK, CHUNK), BF16),
            pltpu.VMEM((2, tm, d_mix), BF16),
        ],
        compiler_params=pltpu.CompilerParams(dimension_semantics=_ARB1, vmem_limit_bytes=limit),
        name="mixer",
    )(sinks, u, v, q, kv, kv, h, g_v, w_s, b_s_t, g_a, g_b, w_out)


def _mem_absorb_kernel(mem_ref, g_ref, wk_ref, wv_ref, wq_ref, wo_ref, qk_ref, vo_ref):
    mn = _rmsnorm(mem_ref[...], g_ref[...]).astype(BF16)
    k = _dot(mn, wk_ref[...].astype(BF16)).astype(BF16)
    v = _dot(mn, wv_ref[...].astype(BF16)).astype(BF16)
    qk_ref[...] = _dot_nt(wq_ref[...], k).astype(BF16)
    vo_ref[...] = _dot(v, wo_ref[...]).astype(BF16)


def _mem_absorb(mem, g, w_kv, wq, wo):
    m, d = mem.shape
    hd = d // X_HEADS
    limit = V7X_VMEM_BYTES
    return pl.pallas_call(
        _mem_absorb_kernel,
        grid=(X_HEADS,),
        in_specs=[
            pl.BlockSpec((m, d), lambda h: (0, 0)),
            pl.BlockSpec((1, d), lambda h: (0, 0)),
            pl.BlockSpec((d, hd), lambda h: (0, h)),
            pl.BlockSpec((d, hd), lambda h: (0, X_HEADS + h)),
            pl.BlockSpec((d, hd), lambda h: (0, h)),
            pl.BlockSpec((hd, d), lambda h: (h, 0)),
        ],
        out_specs=[pl.BlockSpec((d, m), lambda h: (0, h)), pl.BlockSpec((m, d), lambda h: (h, 0))],
        out_shape=[jax.ShapeDtypeStruct((d, X_HEADS * m), BF16),
                   jax.ShapeDtypeStruct((X_HEADS * m, d), BF16)],
        compiler_params=pltpu.CompilerParams(dimension_semantics=_ARB1, vmem_limit_bytes=limit),
        name="mem_absorb",
    )(mem, g, w_kv, w_kv, wq, wo)


def _cross_kernel(h_ref, g_ref, qk_ref, vo_ref, o_ref):
    x = h_ref[...]
    m = qk_ref.shape[1] // X_HEADS
    scale = (x.shape[1] // X_HEADS) ** -0.5
    hn = _rmsnorm(x, g_ref[...]).astype(BF16)
    probs = []
    for head in range(X_HEADS):
        sc = _dot(hn, qk_ref[:, head * m:(head + 1) * m]) * scale
        mx = jnp.max(sc, axis=1, keepdims=True)
        e = jnp.exp(sc - mx)
        probs.append((e * (1.0 / jnp.sum(e, axis=1, keepdims=True))).astype(BF16))
    o_ref[...] = x + _dot(jnp.concatenate(probs, axis=1), vo_ref[...])


def _cross(h, g, qk, vo, *, tm=1024):
    s, d = h.shape
    n = qk.shape[1]
    limit = _vmem_limit(
        4 * _nbytes((tm, d), F32), 2 * 2 * _nbytes((d, n), BF16),
        3 * _nbytes((tm, d), F32), _nbytes((tm, d), BF16), 6 * _nbytes((tm, n), F32))
    return pl.pallas_call(
        _cross_kernel,
        grid=(s // tm,),
        in_specs=[
            pl.BlockSpec((tm, d), lambda i: (i, 0)),
            pl.BlockSpec((1, d), lambda i: (0, 0)),
            pl.BlockSpec((d, n), lambda i: (0, 0)),
            pl.BlockSpec((n, d), lambda i: (0, 0)),
        ],
        out_specs=pl.BlockSpec((tm, d), lambda i: (i, 0)),
        out_shape=jax.ShapeDtypeStruct((s, d), F32),
        compiler_params=pltpu.CompilerParams(dimension_semantics=_ARB1, vmem_limit_bytes=limit),
        name="cross_attn",
    )(h, g, qk, vo)


def _row(v):
    return v.reshape(1, -1)


def kernel(x, mem, g_ffn1, w1_gate, w1_up, w1_down, g_mix, w_in, g_v, w_s, b_s, sinks, g_a_out,
           g_b_out, w_out, g_x, g_mem, w_xq, w_xkv, w_xo, g_ffn2, w2_gate, w2_up, w2_down, g_final):
    depth = g_ffn1.shape[0]
    d_a = g_v.shape[1]
    d_b = g_b_out.shape[1]
    d_kv = 2 * B_KV_HEADS * HEAD_DIM
    outs = []
    bf16_weights = {}
    for b in range(x.shape[0]):
        h = x[b]
        for l in range(depth):
            casts = () if l in bf16_weights else (w_in[l], w_out[l], w_xq[l], w_xo[l])
            h, *copies = _ffn(h, _row(g_ffn1[l]), w1_gate[l], w1_up[l], w1_down[l], _row(g_final),
                              casts, final_norm=False)
            w_in16, w_out16, w_xq16, w_xo16 = bf16_weights.setdefault(l, copies)

            u, v, q, kv = _in_proj(h, _row(g_mix[l]), w_in16, d_a, d_b, d_kv)
            h = _mixer(sinks[l], u, v, q, kv, h, _row(g_v[l]), w_s[l], jnp.transpose(b_s[l]),
                       _row(g_a_out[l]), _row(g_b_out[l]), w_out16)

            qk, vo = _mem_absorb(mem[b], _row(g_mem[l]), w_xkv[l], w_xq16, w_xo16)
            h = _cross(h, _row(g_x[l]), qk, vo)

            h, = _ffn(h, _row(g_ffn2[l]), w2_gate[l], w2_up[l], w2_down[l], _row(g_final),
                      final_norm=(l == depth - 1))
        outs.append(h)
    return outs[0][None] if len(outs) == 1 else jnp.stack(outs, axis=0)
```

```python
import functools

import jax
import jax.numpy as jnp
from jax import lax
from jax.experimental import pallas as pl
from jax.experimental.pallas import tpu as pltpu

F32 = jnp.float32
BF16 = jnp.bfloat16

EPS = 1e-5
NEG = -1e30
CHUNK = 128
A_GROUPS = 8
HEAD_DIM = 64
B_KV_HEADS = 2
X_HEADS = 4

V7X_VMEM_BYTES = 64 * 2**20
V7X_LANES = 128
V7X_MXU_N = 256
CAST_ROWS = 32
PROJ_M = 256
PROJ_N = 256


def _vmem_limit(*byte_counts):
    return int(min(sum(byte_counts), V7X_VMEM_BYTES))


def _nbytes(shape, dtype):
    n = 1
    for s in shape:
        n *= s
    return n * jnp.dtype(dtype).itemsize


def _rmsnorm(x, g):
    ms = jnp.mean(x * x, axis=-1, keepdims=True)
    return x * lax.rsqrt(ms + EPS) * g


def _gelu(x):
    return 0.5 * x * (1.0 + lax.erf(x * (2.0 ** -0.5)))


def _dot(a, b):
    return jnp.dot(a, b, preferred_element_type=F32)


def _dot_nt(a, b):
    return lax.dot_general(a, b, (((1,), (1,)), ((), ())), preferred_element_type=F32)


def _resident(shape):
    return pl.BlockSpec(shape, lambda *_: (0,) * len(shape), pipeline_mode=pl.Buffered(1))


_ARB1 = ("arbitrary",)
_ARB2 = ("arbitrary", "arbitrary")


def _ffn_kernel(h_hbm, g_ref, wg_ref, wu_ref, wd_ref, gf_ref, *rest, down_n, final_norm, n_casts):
    cast_in, o_ref, cast_out = rest[:n_casts], rest[n_casts], rest[n_casts + 1:2 * n_casts + 1]
    hn_ref, hbuf_ref, h_sem = rest[2 * n_casts + 1:]
    i, j = pl.program_id(0), pl.program_id(1)
    tm = hbuf_ref.shape[0]

    def h_copy(tile):
        rows = pl.ds(pl.multiple_of(tile * tm, tm), tm)
        return pltpu.make_async_copy(h_hbm.at[rows, :], hbuf_ref, h_sem)

    for src, dst in zip(cast_in, cast_out):
        dst[...] = src[...].astype(BF16)

    @pl.when(jnp.logical_and(i == 0, j == 0))
    def _():
        h_copy(0).start()

    @pl.when(j == 0)
    def _():
        h_copy(i).wait()
        x = hbuf_ref[...]
        hn_ref[...] = _rmsnorm(x, g_ref[...]).astype(BF16)
        o_ref[...] = x

    @pl.when(jnp.logical_and(j == 1, i + 1 < pl.num_programs(0)))
    def _():
        h_copy(i + 1).start()

    hn = hn_ref[...]
    tf = wg_ref.shape[1]
    act = []
    for c in range(0, tf, V7X_MXU_N):
        cols = slice(c, c + V7X_MXU_N)
        gate = _dot(hn, wg_ref[:, cols].astype(BF16))
        up = _dot(hn, wu_ref[:, cols].astype(BF16))
        act.append(((0.5 * jax.nn.silu(gate)) * up).astype(BF16))
    act = act[0] if len(act) == 1 else jnp.concatenate(act, axis=1)
    for n in range(0, o_ref.shape[1], down_n):
        cols = slice(n, n + down_n)
        o_ref[:, cols] += _dot(act, wd_ref[:, cols].astype(BF16))

    if final_norm:
        @pl.when(j == pl.num_programs(1) - 1)
        def _():
            o_ref[...] = _rmsnorm(o_ref[...], gf_ref[...])


def _ffn(h, g, wg, wu, wd, g_final, casts=(), *, final_norm, tm=1024, tf=512, down_n=512):
    s, d = h.shape
    f = wg.shape[1]
    n_i, n_j = s // tm, f // tf
    band = lambda i, j: (jnp.minimum(i * n_j + j, -(-d // CAST_ROWS) - 1), 0)
    assert all(w.shape[0] == d and d % CAST_ROWS == 0 and d // CAST_ROWS <= n_i * n_j for w in casts)
    cast_specs = lambda: [pl.BlockSpec((CAST_ROWS, w.shape[1]), band) for w in casts]
    limit = _vmem_limit(
        _nbytes((tm, d), F32), 2 * _nbytes((tm, d), F32), _nbytes((tm, d), BF16),
        2 * 3 * _nbytes((d, tf), F32), 3 * _nbytes((d, tf), BF16),
        4 * _nbytes((tm, V7X_MXU_N), F32), _nbytes((tm, tf), BF16),
        2 * _nbytes((tm, down_n), F32), _nbytes((tm, d), F32),
        *[3 * _nbytes((CAST_ROWS, w.shape[1]), F32) for w in casts])
    outs = pl.pallas_call(
        functools.partial(_ffn_kernel, down_n=down_n, final_norm=final_norm, n_casts=len(casts)),
        grid=(n_i, n_j),
        in_specs=[
            pl.BlockSpec(memory_space=pl.ANY),
            pl.BlockSpec((1, d), lambda i, j: (0, 0)),
            pl.BlockSpec((d, tf), lambda i, j: (0, j)),
            pl.BlockSpec((d, tf), lambda i, j: (0, j)),
            pl.BlockSpec((tf, d), lambda i, j: (j, 0)),
            pl.BlockSpec((1, d), lambda i, j: (0, 0)),
            *cast_specs(),
        ],
        out_specs=[pl.BlockSpec((tm, d), lambda i, j: (i, 0)), *cast_specs()],
        out_shape=[jax.ShapeDtypeStruct((s, d), F32),
                   *[jax.ShapeDtypeStruct(w.shape, BF16) for w in casts]],
        scratch_shapes=[pltpu.VMEM((tm, d), BF16), pltpu.VMEM((tm, d), F32),
                        pltpu.SemaphoreType.DMA(())],
        compiler_params=pltpu.CompilerParams(dimension_semantics=_ARB2, vmem_limit_bytes=limit),
        name="ffn",
    )(h, g, wg, wu, wd, g_final, *casts)
    return outs


def _in_proj_kernel(h_ref, g_ref, w_ref, u_ref, v_ref, q_ref, kv_ref):
    hn = _rmsnorm(h_ref[...], g_ref[...]).astype(BF16)
    d_a = u_ref.shape[1]
    d_b = q_ref.shape[1]

    def proj(lo):
        return _dot(hn, w_ref[:, lo:lo + V7X_MXU_N])

    for c in range(0, d_a, V7X_MXU_N):
        u_ref[:, c:c + V7X_MXU_N] = _gelu(proj(c)).astype(BF16)
    for c in range(0, d_a, V7X_MXU_N):
        v_ref[:, c:c + V7X_MXU_N] = _gelu(proj(d_a + c)).astype(BF16)
    for c in range(0, d_b, V7X_MXU_N):
        q_ref[:, c:c + V7X_MXU_N] = proj(2 * d_a + c).astype(BF16)
    for c in range(0, kv_ref.shape[1], V7X_MXU_N):
        kv_ref[:, c:c + V7X_MXU_N] = proj(2 * d_a + d_b + c).astype(BF16)


def _in_proj(h, g, w, d_a, d_b, d_kv, *, tm=1024):
    s, d = h.shape
    n = w.shape[1]
    limit = _vmem_limit(
        2 * _nbytes((tm, d), F32), _nbytes((d, n), BF16), 2 * _nbytes((tm, n), BF16),
        _nbytes((tm, d), F32), _nbytes((tm, d), BF16), 6 * _nbytes((tm, V7X_MXU_N), F32))
    return pl.pallas_call(
        _in_proj_kernel,
        grid=(s // tm,),
        in_specs=[
            pl.BlockSpec((tm, d), lambda i: (i, 0)),
            pl.BlockSpec((1, d), lambda i: (0, 0)),
            _resident((d, n)),
        ],
        out_specs=[
            pl.BlockSpec((tm, d_a), lambda i: (i, 0)),
            pl.BlockSpec((tm, d_a), lambda i: (i, 0)),
            pl.BlockSpec((tm, d_b), lambda i: (i, 0)),
            pl.BlockSpec((tm, d_kv), lambda i: (i, 0)),
        ],
        out_shape=[
            jax.ShapeDtypeStruct((s, d_a), BF16),
            jax.ShapeDtypeStruct((s, d_a), BF16),
            jax.ShapeDtypeStruct((s, d_b), BF16),
            jax.ShapeDtypeStruct((s, d_kv), BF16),
        ],
        compiler_params=pltpu.CompilerParams(dimension_semantics=_ARB1, vmem_limit_bytes=limit),
        name="in_proj",
    )(h, g, w)


def _mixer_kernel(sinks_ref, u_ref, v_ref, q_ref, kvc_ref, kvp_ref, h_ref, gv_ref, ws_ref, bst_ref,
                  ga_ref, gb_ref, wout_ref, o_ref, kvbuf_ref, wsm_ref, y_ref):
    tm, d_a = u_ref.shape
    d_b = q_ref.shape[1]
    n_blocks = tm // CHUNK
    group = d_b // HEAD_DIM // B_KV_HEADS
    kv_w = B_KV_HEADS * HEAD_DIM
    step = pl.program_id(0)
    tile = jnp.minimum(step, pl.num_programs(0) - 2)
    y_new = y_ref.at[step % 2]
    y_old = y_ref.at[(step + 1) % 2]

    @pl.when(step == 0)
    def _():
        y_ref[1] = jnp.zeros(y_ref.shape[1:], BF16)
        t_idx = lax.broadcasted_iota(jnp.int32, (CHUNK, CHUNK), 0)
        s_idx = lax.broadcasted_iota(jnp.int32, (CHUNK, CHUNK), 1)
        for g in range(A_GROUPS):
            wsm_ref[g] = jnp.where(s_idx <= t_idx, ws_ref[g], 0.0).astype(BF16)

    kvbuf_ref[0:CHUNK, :] = kvp_ref[...]
    kvbuf_ref[CHUNK:, :] = kvc_ref[...]

    low_half = lax.broadcasted_iota(jnp.int32, (CHUNK, V7X_LANES), 1) < HEAD_DIM
    low2 = lax.broadcasted_iota(jnp.int32, (2 * CHUNK, V7X_LANES), 1) < HEAD_DIM
    top_half = lax.broadcasted_iota(jnp.int32, (V7X_LANES, CHUNK), 0) < HEAD_DIM
    k_pos = lax.broadcasted_iota(jnp.int32, (2 * CHUNK, CHUNK), 0)
    q_pos = lax.broadcasted_iota(jnp.int32, (2 * CHUNK, CHUNK), 1)
    band = (k_pos > q_pos) & (k_pos <= q_pos + CHUNK)

    pieces = [(r, c) for c in range(0, o_ref.shape[1], PROJ_N) for r in range(0, tm, PROJ_M)]
    n_pieces, n_stages = len(pieces), n_blocks * (1 + B_KV_HEADS * (group // 2))
    stage = [0]

    def project():
        stage[0] += 1
        while n_pieces - len(pieces) < stage[0] * n_pieces // n_stages:
            r, c = pieces.pop(0)
            rows, cols = slice(r, r + PROJ_M), slice(c, c + PROJ_N)
            o_ref[rows, cols] = h_ref[rows, cols] + _dot(y_old[rows, :], wout_ref[:, cols])

    for b in range(n_blocks):
        r0 = b * CHUNK
        blk = slice(r0, r0 + CHUNK)

        project()
        vn = _rmsnorm(v_ref[blk, :].astype(F32), gv_ref[...]).astype(BF16)
        ub = u_ref[blk, :].astype(F32)
        ya = []
        for g in range(A_GROUPS):
            cols = slice(g * CHUNK, (g + 1) * CHUNK)
            sv = _dot(wsm_ref[g], vn[:, cols]) + bst_ref[:, g:g + 1]
            ya.append(ub[:, cols] * sv)
        ya = jnp.concatenate(ya, axis=1)
        y_new[blk, 0:d_a] = _rmsnorm(ya, ga_ref[...]).astype(BF16)

        qb = q_ref[blk, :] * (HEAD_DIM ** -0.5)
        kk = kvbuf_ref[r0:r0 + 2 * CHUNK, :]
        k_f = kk[:, 0:kv_w].astype(F32)
        v_f = kk[:, kv_w:2 * kv_w].astype(F32)
        k_rot = pltpu.roll(k_f, HEAD_DIM, axis=1)
        v_rot = pltpu.roll(v_f, HEAD_DIM, axis=1)
        mask = band & (k_pos >= jnp.where(tile == 0, CHUNK, 0)) if b == 0 else band
        ybt = []
        for j in range(B_KV_HEADS):
            if j == 0:
                k2 = jnp.where(low2, k_f, k_rot).astype(BF16)
                v2t = jnp.where(low2, v_f, v_rot).T.astype(BF16)
            else:
                k2 = jnp.where(low2, k_rot, k_f).astype(BF16)
                v2t = jnp.where(low2, v_rot, v_f).T.astype(BF16)
            qm = []
            for hh in range(group):
                head = j * group + hh
                c0 = (head // 2) * V7X_LANES
                qc = qb[:, c0:c0 + V7X_LANES]
                keep = low_half if head % 2 == 0 else jnp.logical_not(low_half)
                qm.append(jnp.where(keep, qc, jnp.zeros_like(qc)))
            st = _dot_nt(k2, jnp.concatenate(qm, axis=0))
            pt, inv = [], []
            for hh in range(group):
                sink = sinks_ref[j * group + hh]
                sc = jnp.where(mask, st[:, hh * CHUNK:(hh + 1) * CHUNK], NEG)
                m = jnp.maximum(jnp.max(sc, axis=0, keepdims=True), sink)
                e = jnp.exp(sc - m)
                denom = jnp.sum(e, axis=0, keepdims=True) + jnp.exp(sink - m)
                pt.append(e.astype(BF16))
                inv.append(1.0 / denom)
                if hh % 2 == 1:
                    project()
            ot = _dot(v2t, jnp.concatenate(pt, axis=1))
            for pair in range(group // 2):
                even = ot[:, (2 * pair) * CHUNK:(2 * pair + 1) * CHUNK]
                odd = ot[:, (2 * pair + 1) * CHUNK:(2 * pair + 2) * CHUNK]
                scale = jnp.where(top_half, inv[2 * pair], inv[2 * pair + 1])
                ybt.append(jnp.where(top_half, even, odd) * scale)
        ybt = jnp.concatenate(ybt, axis=0)
        ybt = ybt * lax.rsqrt(jnp.mean(ybt * ybt, axis=0, keepdims=True) + EPS)
        y_new[blk, d_a:d_a + d_b] = (ybt.T * gb_ref[...]).astype(BF16)
    assert not pieces


def _mixer(sinks, u, v, q, kv, h, g_v, w_s, b_s_t, g_a, g_b, w_out, *, tm=512):
    s, d = h.shape
    d_a = u.shape[1]
    d_b = q.shape[1]
    d_kv = kv.shape[1]
    d_mix = d_a + d_b
    n_blocks = tm // CHUNK
    n_tiles = s // tm
    limit = _vmem_limit(
        2 * 2 * _nbytes((tm, d_a), BF16), 2 * _nbytes((tm, d_b), BF16),
        2 * _nbytes((tm + CHUNK, d_kv), BF16), 4 * _nbytes((tm, d), F32),
        _nbytes((d_mix, d), BF16), 2 * _nbytes((tm, d_mix), BF16),
        _nbytes((tm + CHUNK, d_kv), BF16), 2 * _nbytes((A_GROUPS, CHUNK, CHUNK), F32),
        2 * _nbytes((tm, d // n_blocks), F32), 32 * _nbytes((CHUNK, d_a), F32))
    vec = lambda n: pl.BlockSpec((1, n), lambda i: (0, 0))
    mixed = lambda i: (jnp.minimum(i, n_tiles - 1), 0)
    projected = lambda i: (jnp.maximum(i - 1, 0), 0)
    prev_block = lambda i: (jnp.maximum(jnp.minimum(i, n_tiles - 1) * n_blocks - 1, 0), 0)
    return pl.pallas_call(
        _mixer_kernel,
        grid=(n_tiles + 1,),
        in_specs=[
            pl.BlockSpec(memory_space=pltpu.SMEM),
            pl.BlockSpec((tm, d_a), mixed),
            pl.BlockSpec((tm, d_a), mixed),
            pl.BlockSpec((tm, d_b), mixed),
            pl.BlockSpec((tm, d_kv), mixed),
            pl.BlockSpec((CHUNK, d_kv), prev_block),
            pl.BlockSpec((tm, d), projected),
            vec(d_a),
            pl.BlockSpec((A_GROUPS, CHUNK, CHUNK), lambda i: (0, 0, 0)),
            pl.BlockSpec((CHUNK, A_GROUPS), lambda i: (0, 0)),
            vec(d_a),
            vec(d_b),
            _resident((d_mix, d)),
        ],
        out_specs=pl.BlockSpec((tm, d), projected),
        out_shape=jax.ShapeDtypeStruct((s, d), F32),
        scratch_shapes=[
            pltpu.VMEM((tm + CHUNK, d_kv), BF16),
            pltpu.VMEM((A_GROUPS, CHUNK, CHUNK), BF16),
            pltpu.VMEM((2, tm, d_mix), BF16),
        ],
        compiler_params=pltpu.CompilerParams(dimension_semantics=_ARB1, vmem_limit_bytes=limit),
        name="mixer",
    )(sinks, u, v, q, kv, kv, h, g_v, w_s, b_s_t, g_a, g_b, w_out)


def _mem_absorb_kernel(mem_ref, g_ref, wk_ref, wv_ref, wq_ref, wo_ref, qk_ref, vo_ref):
    mn = _rmsnorm(mem_ref[...], g_ref[...]).astype(BF16)
    k = _dot(mn, wk_ref[...].astype(BF16)).astype(BF16)
    v = _dot(mn, wv_ref[...].astype(BF16)).astype(BF16)
    qk_ref[...] = _dot_nt(wq_ref[...], k).astype(BF16)
    vo_ref[...] = _dot(v, wo_ref[...]).astype(BF16)


def _mem_absorb(mem, g, w_kv, wq, wo):
    m, d = mem.shape
    hd = d // X_HEADS
    limit = V7X_VMEM_BYTES
    return pl.pallas_call(
        _mem_absorb_kernel,
        grid=(X_HEADS,),
        in_specs=[
            pl.BlockSpec((m, d), lambda h: (0, 0)),
            pl.BlockSpec((1, d), lambda h: (0, 0)),
            pl.BlockSpec((d, hd), lambda h: (0, h)),
            pl.BlockSpec((d, hd), lambda h: (0, X_HEADS + h)),
            pl.BlockSpec((d, hd), lambda h: (0, h)),
            pl.BlockSpec((hd, d), lambda h: (h, 0)),
        ],
        out_specs=[pl.BlockSpec((d, m), lambda h: (0, h)), pl.BlockSpec((m, d), lambda h: (h, 0))],
        out_shape=[jax.ShapeDtypeStruct((d, X_HEADS * m), BF16),
                   jax.ShapeDtypeStruct((X_HEADS * m, d), BF16)],
        compiler_params=pltpu.CompilerParams(dimension_semantics=_ARB1, vmem_limit_bytes=limit),
        name="mem_absorb",
    )(mem, g, w_kv, w_kv, wq, wo)


def _cross_kernel(h_ref, g_ref, qk_ref, vo_ref, o_ref):
    x = h_ref[...]
    m = qk_ref.shape[1] // X_HEADS
    scale = (x.shape[1] // X_HEADS) ** -0.5
    hn = _rmsnorm(x, g_ref[...]).astype(BF16)
    probs = []
    for head in range(X_HEADS):
        sc = _dot(hn, qk_ref[:, head * m:(head + 1) * m]) * scale
        mx = jnp.max(sc, axis=1, keepdims=True)
        e = jnp.exp(sc - mx)
        probs.append((e * (1.0 / jnp.sum(e, axis=1, keepdims=True))).astype(BF16))
    o_ref[...] = x + _dot(jnp.concatenate(probs, axis=1), vo_ref[...])


def _cross(h, g, qk, vo, *, tm=1024):
    s, d = h.shape
    n = qk.shape[1]
    limit = _vmem_limit(
        4 * _nbytes((tm, d), F32), 2 * 2 * _nbytes((d, n), BF16),
        3 * _nbytes((tm, d), F32), _nbytes((tm, d), BF16), 6 * _nbytes((tm, n), F32))
    return pl.pallas_call(
        _cross_kernel,
        grid=(s // tm,),
        in_specs=[
            pl.BlockSpec((tm, d), lambda i: (i, 0)),
            pl.BlockSpec((1, d), lambda i: (0, 0)),
            pl.BlockSpec((d, n), lambda i: (0, 0)),
            pl.BlockSpec((n, d), lambda i: (0, 0)),
        ],
        out_specs=pl.BlockSpec((tm, d), lambda i: (i, 0)),
        out_shape=jax.ShapeDtypeStruct((s, d), F32),
        compiler_params=pltpu.CompilerParams(dimension_semantics=_ARB1, vmem_limit_bytes=limit),
        name="cross_attn",
    )(h, g, qk, vo)


def _row(v):
    return v.reshape(1, -1)


def kernel(x, mem, g_ffn1, w1_gate, w1_up, w1_down, g_mix, w_in, g_v, w_s, b_s, sinks, g_a_out,
           g_b_out, w_out, g_x, g_mem, w_xq, w_xkv, w_xo, g_ffn2, w2_gate, w2_up, w2_down, g_final):
    depth = g_ffn1.shape[0]
    d_a = g_v.shape[1]
    d_b = g_b_out.shape[1]
    d_kv = 2 * B_KV_HEADS * HEAD_DIM
    outs = []
    bf16_weights = {}
    for b in range(x.shape[0]):
        h = x[b]
        for l in range(depth):
            casts = () if l in bf16_weights else (w_in[l], w_out[l], w_xq[l], w_xo[l])
            h, *copies = _ffn(h, _row(g_ffn1[l]), w1_gate[l], w1_up[l], w1_down[l], _row(g_final),
                              casts, final_norm=False)
            w_in16, w_out16, w_xq16, w_xo16 = bf16_weights.setdefault(l, copies)

            u, v, q, kv = _in_proj(h, _row(g_mix[l]), w_in16, d_a, d_b, d_kv)
            h = _mixer(sinks[l], u, v, q, kv, h, _row(g_v[l]), w_s[l], jnp.transpose(b_s[l]),
                       _row(g_a_out[l]), _row(g_b_out[l]), w_out16)

            qk, vo = _mem_absorb(mem[b], _row(g_mem[l]), w_xkv[l], w_xq16, w_xo16)
            h = _cross(h, _row(g_x[l]), qk, vo)

            h, = _ffn(h, _row(g_ffn2[l]), w2_gate[l], w2_up[l], w2_down[l], _row(g_final),
                      final_norm=(l == depth - 1))
        outs.append(h)
    return outs[0][None] if len(outs) == 1 else jnp.stack(outs, axis=0)
```
